```python
import jax, jax.numpy as jnp
from jax import lax
import numpy as np

D_MODEL = 2048
BATCH = 4
SEQ = 4096
DEPTH = 2

N_MIXERS = 2
HGRN_HEADS = 16
HGRN_DK = D_MODEL // HGRN_HEADS
HGRN_DV = D_MODEL // HGRN_HEADS
HGRN_CHUNK = 64
GMLP_CHUNK = 128
GMLP_GROUPS = 16
GMLP_GDIM = D_MODEL // GMLP_GROUPS
D_FF = 4 * D_MODEL
CONV_W = 3
PLE_DIM = 256
EPS = 1e-6
N_HGRN = (DEPTH + 1) // 2
N_GMLP = DEPTH // 2

kernel_name = 'hybrid_hgrn2_gmlp_convffn_trunk'


def rms_norm(x, g):
    xf = x.astype(jnp.float32)
    y = xf * lax.rsqrt(jnp.mean(xf * xf, axis=-1, keepdims=True) + EPS)
    return (y * g.astype(jnp.float32)).astype(x.dtype)


def layer_norm(x, g, b):
    xf = x.astype(jnp.float32)
    mu = jnp.mean(xf, axis=-1, keepdims=True)
    var = jnp.mean(jnp.square(xf - mu), axis=-1, keepdims=True)
    y = (xf - mu) * lax.rsqrt(var + EPS)
    return (y * g.astype(jnp.float32) + b.astype(jnp.float32)).astype(x.dtype)


def hgrn2_chunk_scan(q, k, v, logf):
    B, S, H, DK = q.shape
    DV = v.shape[-1]
    C = HGRN_CHUNK
    N = S // C

    def to_chunks(t):
        return t.astype(jnp.float32).reshape(B, N, C, H, t.shape[-1]).transpose(1, 0, 3, 2, 4)

    qc, kc, vc, gc = to_chunks(q), to_chunks(k), to_chunks(v), to_chunks(logf)
    causal = jnp.tril(jnp.ones((C, C), dtype=bool))[:, :, None]

    def step(state, inp):
        qi, ki, vi, gi = inp
        b = jnp.cumsum(gi, axis=2)
        diff = b[:, :, :, None, :] - b[:, :, None, :, :]
        decay = jnp.exp(jnp.where(causal, diff, -jnp.inf))
        scores = jnp.einsum('bhtsk,bhsk->bhts', qi[:, :, :, None, :] * decay, ki)
        o = (jnp.einsum('bhts,bhsv->bhtv', scores, vi)
             + jnp.einsum('bhtk,bhkv->bhtv', qi * jnp.exp(b), state))
        b_last = b[:, :, -1:, :]
        k_dec = ki * jnp.exp(b_last - b)
        new_state = (state * jnp.exp(b_last[:, :, 0, :])[..., None]
                     + jnp.einsum('bhsk,bhsv->bhkv', k_dec, vi))
        return new_state, o

    init = jnp.zeros((B, H, DK, DV), jnp.float32)
    _, o = lax.scan(step, init, (qc, kc, vc, gc))
    return o.transpose(1, 0, 3, 2, 4).reshape(B, S, H, DV)


def hgrn2_mixer(h, w_in, lb, norm_g, w_out):
    B, S, _ = h.shape
    proj = h @ w_in
    q, fz, i, g = jnp.split(proj, 4, axis=-1)
    heads = lambda t: t.reshape(B, S, HGRN_HEADS, -1)
    lbh = lb.astype(jnp.float32).reshape(HGRN_HEADS, HGRN_DK)
    f = lbh + (1.0 - lbh) * jax.nn.sigmoid(heads(fz).astype(jnp.float32))
    o = hgrn2_chunk_scan(heads(q), 1.0 - f, heads(i), jnp.log(f)).astype(h.dtype)
    o = rms_norm(o, norm_g) * jax.nn.silu(heads(g))
    return o.reshape(B, S, D_MODEL) @ w_out


def gmlp_mixer(h, w_in, ln_g, ln_b, w_s, b_s, w_out):
    B, S, _ = h.shape
    N = S // GMLP_CHUNK
    z = jax.nn.gelu(h @ w_in, approximate=False)
    u, v = jnp.split(z, 2, axis=-1)
    v = layer_norm(v, ln_g, ln_b)
    vc = v.reshape(B, N, GMLP_CHUNK, GMLP_GROUPS, GMLP_GDIM)
    ws = w_s * jnp.tril(jnp.ones((GMLP_CHUNK, GMLP_CHUNK), w_s.dtype))
    mixed = jnp.einsum('gts,bnsgc->bntgc', ws, vc) + b_s.T[:, :, None]
    return (u * mixed.reshape(B, S, D_MODEL)) @ w_out


def conv_ffn(h, w_up, conv_w, conv_b, w_down):
    S = h.shape[1]
    hid = h @ w_up
    padded = jnp.pad(hid, ((0, 0), (CONV_W - 1, 0), (0, 0)))
    acc = conv_b
    for j in range(CONV_W):
        acc = acc + padded[:, j:j + S] * conv_w[j]
    gate, val = jnp.split(acc, 2, axis=-1)
    return (jax.nn.gelu(gate, approximate=True) * val) @ w_down


def setup_inputs(seed: int = 0) -> dict:
    key = jax.random.key(seed)
    ks = jax.random.split(key, 20)
    D = D_MODEL
    nrm = lambda k, shape, scale: jax.random.normal(k, shape, jnp.float32) * scale
    return {
        'x': nrm(ks[0], (BATCH, SEQ, D), 1.0),
        'p': nrm(ks[1], (DEPTH, BATCH, SEQ, PLE_DIM), 1.0),
        'norm_g': 1.0 + nrm(ks[2], (DEPTH, 4, D), 0.02),
        'hgrn_w_in': nrm(ks[3], (N_HGRN, D, 4 * D), D ** -0.5),
        'hgrn_lb_logits': 1.0 + nrm(ks[4], (DEPTH + 1, D), 0.1),
        'hgrn_norm_g': 1.0 + nrm(ks[5], (N_HGRN, HGRN_DV), 0.02),
        'hgrn_w_out': nrm(ks[6], (N_HGRN, D, D), D ** -0.5),
        'gmlp_w_in': nrm(ks[7], (N_GMLP, D, 2 * D), D ** -0.5),
        'gmlp_ln_g': 1.0 + nrm(ks[8], (N_GMLP, D), 0.02),
        'gmlp_ln_b': nrm(ks[9], (N_GMLP, D), 0.02),
        'gmlp_w_s': nrm(ks[10], (N_GMLP, GMLP_GROUPS, GMLP_CHUNK, GMLP_CHUNK), GMLP_CHUNK ** -0.5),
        'gmlp_b_s': 1.0 + nrm(ks[11], (N_GMLP, GMLP_GROUPS, GMLP_CHUNK), 0.02),
        'gmlp_w_out': nrm(ks[12], (N_GMLP, D, D), D ** -0.5),
        'ffn_w_up': nrm(ks[13], (DEPTH, D, 2 * D_FF), D ** -0.5),
        'ffn_conv_w': nrm(ks[14], (DEPTH, CONV_W, 2 * D_FF), CONV_W ** -0.5),
        'ffn_conv_b': nrm(ks[15], (DEPTH, 2 * D_FF), 0.01),
        'ffn_w_down': nrm(ks[16], (DEPTH, D_FF, D), D_FF ** -0.5),
        'ple_w_in': nrm(ks[17], (DEPTH, PLE_DIM, D), PLE_DIM ** -0.5),
        'ple_w_gate': nrm(ks[18], (DEPTH, D, D), D ** -0.5),
        'ple_norm_g': 1.0 + nrm(ks[19], (DEPTH, 2, D), 0.02),
    }


def reference(x, p, norm_g, hgrn_w_in, hgrn_lb_logits, hgrn_norm_g, hgrn_w_out,
              gmlp_w_in, gmlp_ln_g, gmlp_ln_b, gmlp_w_s, gmlp_b_s, gmlp_w_out,
              ffn_w_up, ffn_conv_w, ffn_conv_b, ffn_w_down,
              ple_w_in, ple_w_gate, ple_norm_g):
    lb_all = jnp.cumsum(jax.nn.softmax(hgrn_lb_logits.astype(jnp.float32), axis=0), axis=0)
    r = x
    for i in range(DEPTH):
        j = i // N_MIXERS
        hn = rms_norm(r, norm_g[i, 0])
        if i % N_MIXERS == 0:
            m = hgrn2_mixer(hn, hgrn_w_in[j], lb_all[i], hgrn_norm_g[j], hgrn_w_out[j])
        else:
            m = gmlp_mixer(hn, gmlp_w_in[j], gmlp_ln_g[j], gmlp_ln_b[j],
                           gmlp_w_s[j], gmlp_b_s[j], gmlp_w_out[j])
        r = r + rms_norm(m, norm_g[i, 1])
        f = conv_ffn(rms_norm(r, norm_g[i, 2]), ffn_w_up[i], ffn_conv_w[i],
                     ffn_conv_b[i], ffn_w_down[i])
        r = r + rms_norm(f, norm_g[i, 3])
        e = rms_norm(p[i] @ ple_w_in[i], ple_norm_g[i, 0])
        gate = jax.nn.sigmoid(rms_norm(r, ple_norm_g[i, 1]) @ ple_w_gate[i])
        r = r + gate * e
    return r
```

```python
import functools

import jax
import jax.numpy as jnp
from jax import lax
from jax.experimental import pallas as pl
from jax.experimental.pallas import tpu as pltpu

EPS = 1e-6
LANES = 128
HGRN_CHUNK = 64
GMLP_CHUNK = 128
CONV_W = 3
HALO = 16
VMEM_LIMIT = 56 * 1024 * 1024

_BF = jnp.bfloat16
_F32 = jnp.float32


def _params(n_axes):
    return pltpu.CompilerParams(
        dimension_semantics=("arbitrary",) * n_axes,
        vmem_limit_bytes=VMEM_LIMIT,
    )


def _resident(shape, index_map):
    return pl.BlockSpec(shape, index_map, pipeline_mode=pl.Buffered(1))


def _rms(x, g):
    return x * lax.rsqrt(jnp.mean(x * x, axis=-1, keepdims=True) + EPS) * g


def _sigmoid(x):
    return 1.0 / (1.0 + jnp.exp(-x))


def _dot(a, b):
    return jnp.dot(a, b, preferred_element_type=_F32)


def _dot_nt(a, b):
    return lax.dot_general(a, b, (((1,), (1,)), ((), ())), preferred_element_type=_F32)


def _dot_tn(a, b):
    return lax.dot_general(a, b, (((0,), (0,)), ((), ())), preferred_element_type=_F32)


def _chunk_cumsum(x, pos):
    shift = 1
    while shift < HGRN_CHUNK:
        x = x + jnp.where(pos >= shift, pltpu.roll(x, shift, axis=0), 0.0)
        shift *= 2
    return x


def _hgrn_kernel(r_ref, gpre_ref, win_ref, lbl_ref, ng_ref, y_ref, hn_ref, st_ref, *, layer, tm):
    n = pl.program_id(1)
    h = pl.program_id(2)

    @pl.when(h == 0)
    def _():
        hn_ref[...] = _rms(r_ref[...], gpre_ref[...]).astype(_BF)

    @pl.when(n == 0)
    def _():
        st_ref[h] = jnp.zeros((LANES, LANES), _F32)

    proj = _dot(hn_ref[...], win_ref[...])
    q = proj[:, 0:LANES]
    fz = proj[:, LANES:2 * LANES]
    v = proj[:, 2 * LANES:3 * LANES]
    og = proj[:, 3 * LANES:4 * LANES]

    lg = lbl_ref[...]
    ex = jnp.exp(lg - jnp.max(lg, axis=0, keepdims=True))
    lb = jnp.sum(ex[0:layer + 1], axis=0, keepdims=True) / jnp.sum(ex, axis=0, keepdims=True)

    f = lb + (1.0 - lb) * _sigmoid(fz)
    k = 1.0 - f
    pos = lax.broadcasted_iota(jnp.int32, (tm, LANES), 0) % HGRN_CHUNK
    bcum = _chunk_cumsum(jnp.log(f), pos)

    row = lax.broadcasted_iota(jnp.int32, (HGRN_CHUNK, HGRN_CHUNK), 0)
    col = lax.broadcasted_iota(jnp.int32, (HGRN_CHUNK, HGRN_CHUNK), 1)
    causal = row >= col

    st = st_ref[h]
    outs = []
    for c in range(tm // HGRN_CHUNK):
        sl = slice(c * HGRN_CHUNK, (c + 1) * HGRN_CHUNK)
        b = bcum[sl]
        qc, kc, vc = q[sl], k[sl], v[sl]
        b_mid = b[HGRN_CHUNK // 2 - 1:HGRN_CHUNK // 2]
        b_last = b[HGRN_CHUNK - 1:HGRN_CHUNK]
        q_mid = (qc * jnp.exp(b - b_mid)).astype(_BF)
        k_mid = (kc * jnp.exp(b_mid - b)).astype(_BF)
        scores = jnp.where(causal, _dot_nt(q_mid, k_mid), 0.0)
        vb = vc.astype(_BF)
        o = _dot(scores.astype(_BF), vb) + _dot_nt((qc * jnp.exp(b)).astype(_BF), st.astype(_BF))
        k_dec = (kc * jnp.exp(b_last - b)).astype(_BF)
        st = st * jnp.exp(b_last) + _dot_tn(vb, k_dec)
        outs.append(o)
    st_ref[h] = st

    o = jnp.concatenate(outs, axis=0)
    o = _rms(o, ng_ref[...])
    y_ref[...] = (o * (og * _sigmoid(og))).astype(_BF)


def _hgrn_mixer(r, g_pre, w_in_heads, lb_logits_heads, norm_g, *, layer, batch, seq):
    T, D = r.shape
    H = D // LANES
    tm = min(512, seq)
    n_s = seq // tm
    kern = functools.partial(_hgrn_kernel, layer=layer, tm=tm)
    return pl.pallas_call(
        kern,
        grid=(batch, n_s, H),
        in_specs=[
            pl.BlockSpec((tm, D), lambda b, n, h: (b * n_s + n, 0)),
            _resident((1, D), lambda b, n, h: (0, 0)),
            pl.BlockSpec((None, D, 4 * LANES), lambda b, n, h: (h, 0, 0)),
            pl.BlockSpec((None,) + lb_logits_heads.shape[1:], lambda b, n, h: (h, 0, 0)),
            _resident((1, LANES), lambda b, n, h: (0, 0)),
        ],
        out_specs=pl.BlockSpec((tm, LANES), lambda b, n, h: (b * n_s + n, h)),
        out_shape=jax.ShapeDtypeStruct((T, D), _BF),
        scratch_shapes=[
            pltpu.VMEM((tm, D), _BF),
            pltpu.VMEM((H, LANES, LANES), _F32),
        ],
        compiler_params=_params(3),
        name="hgrn_mixer",
    )(r, g_pre, w_in_heads, lb_logits_heads, norm_g)


def _gelu_erf(x):
    return 0.5 * x * (1.0 + lax.erf(x * (2.0 ** -0.5)))


def _gmlp_kernel(r_ref, gpre_ref, win_ref, lng_ref, lnb_ref, ws_ref, bst_ref, y_ref, *, tm):
    D = r_ref.shape[1]
    hn = _rms(r_ref[...], gpre_ref[...]).astype(_BF)
    u = _gelu_erf(_dot(hn, win_ref[:, 0:D]))
    v = _gelu_erf(_dot(hn, win_ref[:, D:2 * D]))
    mu = jnp.mean(v, axis=-1, keepdims=True)
    vc = v - mu
    var = jnp.mean(vc * vc, axis=-1, keepdims=True)
    vb = (vc * lax.rsqrt(var + EPS) * lng_ref[...] + lnb_ref[...]).astype(_BF)

    row = lax.broadcasted_iota(jnp.int32, (GMLP_CHUNK, GMLP_CHUNK), 0)
    col = lax.broadcasted_iota(jnp.int32, (GMLP_CHUNK, GMLP_CHUNK), 1)
    causal = row >= col
    for g in range(D // LANES):
        cs = slice(g * LANES, (g + 1) * LANES)
        w = jnp.where(causal, ws_ref[g], 0.0).astype(_BF)
        bias = bst_ref[:, g:g + 1]
        for c in range(tm // GMLP_CHUNK):
            rs = slice(c * GMLP_CHUNK, (c + 1) * GMLP_CHUNK)
            mixed = _dot(w, vb[rs, cs]) + bias
            y_ref[rs, cs] = (u[rs, cs] * mixed).astype(_BF)


def _gmlp_mixer(r, g_pre, w_in, ln_g, ln_b, w_s, b_s_t):
    T, D = r.shape
    G = D // LANES
    tm = 256
    return pl.pallas_call(
        functools.partial(_gmlp_kernel, tm=tm),
        grid=(T // tm,),
        in_specs=[
            pl.BlockSpec((tm, D), lambda i: (i, 0)),
            _resident((1, D), lambda i: (0, 0)),
            _resident((D, 2 * D), lambda i: (0, 0)),
            _resident((1, D), lambda i: (0, 0)),
            _resident((1, D), lambda i: (0, 0)),
            _resident((G, GMLP_CHUNK, GMLP_CHUNK), lambda i: (0, 0, 0)),
            _resident((GMLP_CHUNK, G), lambda i: (0, 0)),
        ],
        out_specs=pl.BlockSpec((tm, D), lambda i: (i, 0)),
        out_shape=jax.ShapeDtypeStruct((T, D), _BF),
        compiler_params=_params(1),
        name="gmlp_mixer",
    )(r, g_pre, w_in, ln_g, ln_b, w_s, b_s_t)


def _proj_kernel(y_ref, w_ref, r_ref, g_ref, out_ref):
    out_ref[...] = r_ref[...] + _rms(_dot(y_ref[...], w_ref[...]), g_ref[...])


def _proj_residual(y, w, r, g_post):
    T, D = r.shape
    tm = 512
    return pl.pallas_call(
        _proj_kernel,
        grid=(T // tm,),
        in_specs=[
            pl.BlockSpec((tm, D), lambda i: (i, 0)),
            _resident((D, D), lambda i: (0, 0)),
            pl.BlockSpec((tm, D), lambda i: (i, 0)),
            _resident((1, D), lambda i: (0, 0)),
        ],
        out_specs=pl.BlockSpec((tm, D), lambda i: (i, 0)),
        out_shape=jax.ShapeDtypeStruct((T, D), _F32),
        compiler_params=_params(1),
        name="proj_residual",
    )(y, w, r, g_post)


def _gelu_tanh(x):
    return 0.5 * x * (1.0 + jnp.tanh((2.0 / jnp.pi) ** 0.5 * (x + 0.044715 * (x * x * x))))


def _ffn_kernel(r_ref, halo_ref, gpre_ref, gpost_ref, wg_ref, wv_ref, cwg_ref, cwv_ref,
                cbg_ref, cbv_ref, wd_ref, out_ref, hn_ref, hg_ref, hv_ref, *, tm, tiles_per_seq):
    i = pl.program_id(0)
    c = pl.program_id(1)

    @pl.when(c == 0)
    def _():
        hn_ref[pl.ds(HALO, tm), :] = _rms(r_ref[...], gpre_ref[...]).astype(_BF)
        halo = _rms(halo_ref[...], gpre_ref[...])
        halo = jnp.where(i % tiles_per_seq == 0, 0.0, halo)
        hn_ref[pl.ds(0, HALO), :] = halo.astype(_BF)

    hg_ref[...] = _dot(hn_ref[...], wg_ref[...])
    hv_ref[...] = _dot(hn_ref[...], wv_ref[...])

    def conv(h_ref, cw_ref, cb_ref):
        acc = cb_ref[...] + h_ref[pl.ds(HALO, tm), :] * cw_ref[CONV_W - 1:CONV_W, :]
        for j in range(1, CONV_W):
            acc = acc + h_ref[pl.ds(HALO - j, tm), :] * cw_ref[CONV_W - 1 - j:CONV_W - j, :]
        return acc

    act = (_gelu_tanh(conv(hg_ref, cwg_ref, cbg_ref)) * conv(hv_ref, cwv_ref, cbv_ref)).astype(_BF)
    down = _dot(act, wd_ref[...])

    @pl.when(c == 0)
    def _():
        out_ref[...] = down

    @pl.when(c > 0)
    def _():
        out_ref[...] += down

    @pl.when(c == pl.num_programs(1) - 1)
    def _():
        out_ref[...] = r_ref[...] + _rms(out_ref[...], gpost_ref[...])


def _conv_ffn(r, g_pre, g_post, w_up, conv_w, conv_b, w_down, *, seq):
    T, D = r.shape
    F = w_down.shape[0]
    tm = min(512, seq)
    fc = min(512, F)
    n_f = F // fc
    kern = functools.partial(_ffn_kernel, tm=tm, tiles_per_seq=seq // tm)
    return pl.pallas_call(
        kern,
        grid=(T // tm, n_f),
        in_specs=[
            _resident((tm, D), lambda i, c: (i, 0)),
            pl.BlockSpec((HALO, D), lambda i, c: (jnp.maximum(i * (tm // HALO) - 1, 0), 0)),
            _resident((1, D), lambda i, c: (0, 0)),
            _resident((1, D), lambda i, c: (0, 0)),
            pl.BlockSpec((D, fc), lambda i, c: (0, c)),
            pl.BlockSpec((D, fc), lambda i, c: (0, c + n_f)),
            pl.BlockSpec((CONV_W, fc), lambda i, c: (0, c)),
            pl.BlockSpec((CONV_W, fc), lambda i, c: (0, c + n_f)),
            pl.BlockSpec((1, fc), lambda i, c: (0, c)),
            pl.BlockSpec((1, fc), lambda i, c: (0, c + n_f)),
            pl.BlockSpec((fc, D), lambda i, c: (c, 0)),
        ],
        out_specs=pl.BlockSpec((tm, D), lambda i, c: (i, 0)),
        out_shape=jax.ShapeDtypeStruct((T, D), _F32),
        scratch_shapes=[
            pltpu.VMEM((tm + HALO, D), _BF),
            pltpu.VMEM((tm + HALO, fc), _F32),
            pltpu.VMEM((tm + HALO, fc), _F32),
        ],
        compiler_params=_params(2),
        name="conv_ffn",
    )(r, r, g_pre, g_post, w_up, w_up, conv_w, conv_w, conv_b, conv_b, w_down)


def _ple_kernel(r_ref, p_ref, win_ref, wgate_ref, ge_ref, gr_ref, out_ref):
    r = r_ref[...]
    e = _rms(_dot(p_ref[...].astype(_BF), win_ref[...]), ge_ref[...])
    gate = _sigmoid(_dot(_rms(r, gr_ref[...]).astype(_BF), wgate_ref[...]))
    out_ref[...] = r + gate * e


def _ple_gate(r, p, w_in, w_gate, g_e, g_r):
    T, D = r.shape
    P = p.shape[1]
    tm = 512
    return pl.pallas_call(
        _ple_kernel,
        grid=(T // tm,),
        in_specs=[
            pl.BlockSpec((tm, D), lambda i: (i, 0)),
            pl.BlockSpec((tm, P), lambda i: (i, 0)),
            _resident((P, D), lambda i: (0, 0)),
            _resident((D, D), lambda i: (0, 0)),
            _resident((1, D), lambda i: (0, 0)),
            _resident((1, D), lambda i: (0, 0)),
        ],
        out_specs=pl.BlockSpec((tm, D), lambda i: (i, 0)),
        out_shape=jax.ShapeDtypeStruct((T, D), _F32),
        compiler_params=_params(1),
        name="ple_gate",
    )(r, p, w_in, w_gate, g_e, g_r)


def kernel(x, p, norm_g, hgrn_w_in, hgrn_lb_logits, hgrn_norm_g, hgrn_w_out, gmlp_w_in, gmlp_ln_g,
           gmlp_ln_b, gmlp_w_s, gmlp_b_s, gmlp_w_out, ffn_w_up, ffn_conv_w, ffn_conv_b, ffn_w_down,
           ple_w_in, ple_w_gate, ple_norm_g):
    B, S, D = x.shape
    depth = p.shape[0]
    H = D // LANES
    T = B * S
    row = lambda a: a.reshape(1, -1)

    lbl = hgrn_lb_logits.astype(_F32).reshape(-1, H, LANES).transpose(1, 0, 2)

    r = x.reshape(T, D)
    for i in range(depth):
        j = i // 2
        if i % 2 == 0:
            w_in = hgrn_w_in[j].astype(_BF).reshape(D, 4, H, LANES).transpose(2, 0, 1, 3)
            w_in = w_in.reshape(H, D, 4 * LANES)
            y = _hgrn_mixer(r, row(norm_g[i, 0]), w_in, lbl, row(hgrn_norm_g[j]),
                            layer=i, batch=B, seq=S)
            w_out = hgrn_w_out[j]
        else:
            y = _gmlp_mixer(r, row(norm_g[i, 0]), gmlp_w_in[j].astype(_BF), row(gmlp_ln_g[j]),
                            row(gmlp_ln_b[j]), gmlp_w_s[j], gmlp_b_s[j].T)
            w_out = gmlp_w_out[j]
        r = _proj_residual(y, w_out.astype(_BF), r, row(norm_g[i, 1]))
        r = _conv_ffn(r, row(norm_g[i, 2]), row(norm_g[i, 3]), ffn_w_up[i].astype(_BF),
                      ffn_conv_w[i], row(ffn_conv_b[i]), ffn_w_down[i].astype(_BF), seq=S)
        r = _ple_gate(r, p[i].reshape(T, -1), ple_w_in[i].astype(_BF), ple_w_gate[i].astype(_BF),
                      row(ple_norm_g[i, 0]), row(ple_norm_g[i, 1]))
    return r.reshape(B, S, D)
```

```python
import functools

import jax
import jax.numpy as jnp
from jax import lax
from jax.experimental import pallas as pl
from jax.experimental.pallas import tpu as pltpu

EPS = 1e-6
LANES = 128
HGRN_CHUNK = 64
HGRN_SAFE_DECAY = 60.0
GMLP_CHUNK = 128
CONV_W = 3
HALO = 16
VMEM_LIMIT = 60 * 1024 * 1024

_BF = jnp.bfloat16
_F32 = jnp.float32


def _params(n_axes):
    return pltpu.CompilerParams(
        dimension_semantics=("arbitrary",) * n_axes,
        vmem_limit_bytes=VMEM_LIMIT,
    )


def _resident(shape, index_map):
    return pl.BlockSpec(shape, index_map, pipeline_mode=pl.Buffered(1))


def _rms(x, g):
    return x * lax.rsqrt(jnp.mean(x * x, axis=-1, keepdims=True) + EPS) * g


def _sigmoid(x):
    return 1.0 / (1.0 + jnp.exp(-x))


def _dot(a, b):
    return jnp.dot(a, b, preferred_element_type=_F32)


def _dot_nt(a, b):
    return lax.dot_general(a, b, (((1,), (1,)), ((), ())), preferred_element_type=_F32)


def _dot_tn(a, b):
    return lax.dot_general(a, b, (((0,), (0,)), ((), ())), preferred_element_type=_F32)


def _chunk_cumsum(x, pos):
    shift = 1
    while shift < HGRN_CHUNK:
        x = x + jnp.where(pos >= shift, pltpu.roll(x, shift, axis=0), 0.0)
        shift *= 2
    return x


def _hgrn_gates(proj, lb):
    q = proj[:, 0:LANES]
    f = lb + (1.0 - lb) * _sigmoid(proj[:, LANES:2 * LANES])
    v = proj[:, 2 * LANES:3 * LANES]
    og = proj[:, 3 * LANES:4 * LANES]
    pos = lax.broadcasted_iota(jnp.int32, f.shape, 0) % HGRN_CHUNK
    return q, 1.0 - f, v, og, _chunk_cumsum(jnp.log(f), pos)


def _hgrn_out(o, og, ng):
    return (_rms(o, ng) * (og * _sigmoid(og))).astype(_BF)


def _hgrn_kernel(r_ref, gpre_ref, win_ref, lbl_ref, ng_ref, y_ref, hn_ref, st_ref, proj_ref, o_ref,
                 fq_ref, fb_ref, *, layer, tm, rb):
    n = pl.program_id(1)
    h = pl.program_id(2)

    @pl.when(h == 0)
    def _():
        hn_ref[...] = _rms(r_ref[...], gpre_ref[...]).astype(_BF)

    @pl.when(n == 0)
    def _():
        st_ref[h] = jnp.zeros((LANES, LANES), _F32)

    lg = lbl_ref[...]
    ex = jnp.exp(lg - jnp.max(lg, axis=0, keepdims=True))
    lb = jnp.sum(ex[0:layer + 1], axis=0, keepdims=True) / jnp.sum(ex, axis=0, keepdims=True)

    n_blocks = tm // rb
    for qb in range(n_blocks):
        rows = pl.ds(qb * rb, rb)
        proj_ref[rows, :] = _dot(hn_ref[rows, :], win_ref[...])

    row = lax.broadcasted_iota(jnp.int32, (HGRN_CHUNK, HGRN_CHUNK), 0)
    col = lax.broadcasted_iota(jnp.int32, (HGRN_CHUNK, HGRN_CHUNK), 1)
    causal = row >= col

    st = st_ref[h]
    any_unsafe = jnp.zeros((1, LANES), _F32)
    for qb in range(n_blocks):
        rows = pl.ds(qb * rb, rb)
        q, k, v, og, bcum = _hgrn_gates(proj_ref[rows, :], lb)
        outs = []
        for c in range(rb // HGRN_CHUNK):
            sl = slice(c * HGRN_CHUNK, (c + 1) * HGRN_CHUNK)
            b = bcum[sl]
            qc, kc, vc = q[sl], k[sl], v[sl]
            b_mid = b[HGRN_CHUNK // 2 - 1:HGRN_CHUNK // 2]
            b_last = b[HGRN_CHUNK - 1:HGRN_CHUNK]
            unsafe = b_last < -HGRN_SAFE_DECAY
            any_unsafe = jnp.maximum(any_unsafe, unsafe.astype(_F32))
            q_mid = jnp.where(unsafe, 0.0, qc * jnp.exp(b - b_mid)).astype(_BF)
            k_mid = jnp.where(unsafe, 0.0, kc * jnp.exp(b_mid - b)).astype(_BF)
            scores = jnp.where(causal, _dot_nt(q_mid, k_mid), 0.0)
            vb = vc.astype(_BF)
            o = _dot(scores.astype(_BF), vb) + _dot_nt((qc * jnp.exp(b)).astype(_BF), st.astype(_BF))
            k_dec = (kc * jnp.exp(b_last - b)).astype(_BF)
            st = st * jnp.exp(b_last) + _dot_tn(vb, k_dec)
            outs.append(o)
        o = jnp.concatenate(outs, axis=0)
        o_ref[rows, :] = o
        y_ref[rows, :] = _hgrn_out(o, og, ng_ref[...])
    st_ref[h] = st

    @pl.when(jnp.max(any_unsafe) > 0.0)
    def _():
        s_idx = lax.broadcasted_iota(jnp.int32, (HGRN_CHUNK, 1), 0)

        def chunk_body(c, carry):
            base = pl.multiple_of(c * HGRN_CHUNK, HGRN_CHUNK)
            q, k, v, og, b = _hgrn_gates(proj_ref[pl.ds(base, HGRN_CHUNK), :], lb)
            unsafe = b[HGRN_CHUNK - 1:HGRN_CHUNK] < -HGRN_SAFE_DECAY
            fq_ref[...] = jnp.where(unsafe, q, 0.0)
            fb_ref[...] = b

            def row_body(t, carry2):
                qt = fq_ref[pl.ds(t, 1), :]
                bt = fb_ref[pl.ds(t, 1), :]
                w = qt * k * jnp.exp(jnp.minimum(bt - b, 0.0))
                w = jnp.where(s_idx <= t, w, 0.0)
                sc = jnp.sum(w, axis=1, keepdims=True)
                o_ref[pl.ds(base + t, 1), :] += jnp.sum(sc * v, axis=0, keepdims=True)
                return carry2

            lax.fori_loop(0, HGRN_CHUNK, row_body, 0)
            y_ref[pl.ds(base, HGRN_CHUNK), :] = _hgrn_out(o_ref[pl.ds(base, HGRN_CHUNK), :], og, ng_ref[...])
            return carry

        lax.fori_loop(0, tm // HGRN_CHUNK, chunk_body, 0)


def _hgrn_mixer(r, g_pre, w_in_heads, lb_logits_heads, norm_g, *, layer, batch, seq):
    T, D = r.shape
    H = D // LANES
    tm = min(1024, seq)
    rb = min(256, tm)
    n_s = seq // tm
    kern = functools.partial(_hgrn_kernel, layer=layer, tm=tm, rb=rb)
    return pl.pallas_call(
        kern,
        grid=(batch, n_s, H),
        in_specs=[
            pl.BlockSpec((tm, D), lambda b, n, h: (b * n_s + n, 0)),
            _resident((1, D), lambda b, n, h: (0, 0)),
            pl.BlockSpec((None, D, 4 * LANES), lambda b, n, h: (h, 0, 0)),
            pl.BlockSpec((None,) + lb_logits_heads.shape[1:], lambda b, n, h: (h, 0, 0)),
            _resident((1, LANES), lambda b, n, h: (0, 0)),
        ],
        out_specs=pl.BlockSpec((tm, LANES), lambda b, n, h: (b * n_s + n, h)),
        out_shape=jax.ShapeDtypeStruct((T, D), _BF),
        scratch_shapes=[
            pltpu.VMEM((tm, D), _BF),
            pltpu.VMEM((H, LANES, LANES), _F32),
            pltpu.VMEM((tm, 4 * LANES), _F32),
            pltpu.VMEM((tm, LANES), _F32),
            pltpu.VMEM((HGRN_CHUNK, LANES), _F32),
            pltpu.VMEM((HGRN_CHUNK, LANES), _F32),
        ],
        compiler_params=_params(3),
        name="hgrn_mixer",
    )(r, g_pre, w_in_heads, lb_logits_heads, norm_g)


def _gelu_erf(x):
    return 0.5 * x * (1.0 + lax.erf(x * (2.0 ** -0.5)))


def _gmlp_kernel(r_ref, gpre_ref, win_ref, lng_ref, lnb_ref, ws_ref, bst_ref, y_ref, *, tm):
    D = r_ref.shape[1]
    hn = _rms(r_ref[...], gpre_ref[...]).astype(_BF)
    u = _gelu_erf(_dot(hn, win_ref[:, 0:D]))
    v = _gelu_erf(_dot(hn, win_ref[:, D:2 * D]))
    mu = jnp.mean(v, axis=-1, keepdims=True)
    vc = v - mu
    var = jnp.mean(vc * vc, axis=-1, keepdims=True)
    vb = (vc * lax.rsqrt(var + EPS) * lng_ref[...] + lnb_ref[...]).astype(_BF)

    row = lax.broadcasted_iota(jnp.int32, (GMLP_CHUNK, GMLP_CHUNK), 0)
    col = lax.broadcasted_iota(jnp.int32, (GMLP_CHUNK, GMLP_CHUNK), 1)
    causal = row >= col
    for g in range(D // LANES):
        cs = slice(g * LANES, (g + 1) * LANES)
        w = jnp.where(causal, ws_ref[g], 0.0).astype(_BF)
        bias = bst_ref[:, g:g + 1]
        for c in range(tm // GMLP_CHUNK):
            rs = slice(c * GMLP_CHUNK, (c + 1) * GMLP_CHUNK)
            mixed = _dot(w, vb[rs, cs]) + bias
            y_ref[rs, cs] = (u[rs, cs] * mixed).astype(_BF)


def _gmlp_mixer(r, g_pre, w_in, ln_g, ln_b, w_s, b_s_t):
    T, D = r.shape
    G = D // LANES
    tm = 256
    return pl.pallas_call(
        functools.partial(_gmlp_kernel, tm=tm),
        grid=(T // tm,),
        in_specs=[
            pl.BlockSpec((tm, D), lambda i: (i, 0)),
            _resident((1, D), lambda i: (0, 0)),
            _resident((D, 2 * D), lambda i: (0, 0)),
            _resident((1, D), lambda i: (0, 0)),
            _resident((1, D), lambda i: (0, 0)),
            _resident((G, GMLP_CHUNK, GMLP_CHUNK), lambda i: (0, 0, 0)),
            _resident((GMLP_CHUNK, G), lambda i: (0, 0)),
        ],
        out_specs=pl.BlockSpec((tm, D), lambda i: (i, 0)),
        out_shape=jax.ShapeDtypeStruct((T, D), _BF),
        compiler_params=_params(1),
        name="gmlp_mixer",
    )(r, g_pre, w_in, ln_g, ln_b, w_s, b_s_t)


def _proj_kernel(y_ref, w_ref, r_ref, g_ref, out_ref):
    out_ref[...] = r_ref[...] + _rms(_dot(y_ref[...], w_ref[...]), g_ref[...])


def _proj_residual(y, w, r, g_post):
    T, D = r.shape
    tm = 512
    return pl.pallas_call(
        _proj_kernel,
        grid=(T // tm,),
        in_specs=[
            pl.BlockSpec((tm, D), lambda i: (i, 0)),
            _resident((D, D), lambda i: (0, 0)),
            pl.BlockSpec((tm, D), lambda i: (i, 0)),
            _resident((1, D), lambda i: (0, 0)),
        ],
        out_specs=pl.BlockSpec((tm, D), lambda i: (i, 0)),
        out_shape=jax.ShapeDtypeStruct((T, D), _F32),
        compiler_params=_params(1),
        name="proj_residual",
    )(y, w, r, g_post)


def _gelu_tanh(x):
    return 0.5 * x * (1.0 + jnp.tanh((2.0 / jnp.pi) ** 0.5 * (x + 0.044715 * (x * x * x))))


def _ffn_kernel(r_ref, halo_ref, gpre_ref, gpost_ref, wg_ref, wv_ref, cwg_ref, cwv_ref,
                cbg_ref, cbv_ref, wd_ref, out_ref, hn_ref, hg_ref, hv_ref, *, tm, rb, tiles_per_seq):
    i = pl.program_id(0)
    c = pl.program_id(1)

    @pl.when(c == 0)
    def _():
        hn_ref[pl.ds(HALO, tm), :] = _rms(r_ref[...], gpre_ref[...]).astype(_BF)
        halo = _rms(halo_ref[...], gpre_ref[...])
        halo = jnp.where(i % tiles_per_seq == 0, 0.0, halo)
        hn_ref[pl.ds(0, HALO), :] = halo.astype(_BF)
        out_ref[...] = jnp.zeros(out_ref.shape, _F32)

    n_blocks = tm // rb
    for q in range(n_blocks):
        for h_ref, w_ref in ((hg_ref, wg_ref), (hv_ref, wv_ref)):
            if q == 0:
                h_ref[0] = _dot(hn_ref[pl.ds(0, HALO + rb), :], w_ref[...])
            else:
                h_ref[q, pl.ds(HALO, rb), :] = _dot(hn_ref[pl.ds(HALO + q * rb, rb), :], w_ref[...])
                h_ref[q, pl.ds(HALO - 8, 8), :] = h_ref[q - 1, pl.ds(HALO + rb - 8, 8), :]

    def conv(h_ref, q, cw_ref, cb_ref):
        acc = cb_ref[...] + h_ref[q, pl.ds(HALO, rb), :] * cw_ref[CONV_W - 1:CONV_W, :]
        for j in range(1, CONV_W):
            acc = acc + h_ref[q, pl.ds(HALO - j, rb), :] * cw_ref[CONV_W - 1 - j:CONV_W - j, :]
        return acc

    for q in range(n_blocks):
        gate = conv(hg_ref, q, cwg_ref, cbg_ref)
        val = conv(hv_ref, q, cwv_ref, cbv_ref)
        act = (_gelu_tanh(gate) * val).astype(_BF)
        out_ref[pl.ds(q * rb, rb), :] += _dot(act, wd_ref[...])

    @pl.when(c == pl.num_programs(1) - 1)
    def _():
        out_ref[...] = r_ref[...] + _rms(out_ref[...], gpost_ref[...])


def _conv_ffn(r, g_pre, g_post, w_up, conv_w, conv_b, w_down, *, seq):
    T, D = r.shape
    F = w_down.shape[0]
    tm = min(1024, seq)
    rb = min(256, tm)
    fc = min(512, F)
    n_f = F // fc
    kern = functools.partial(_ffn_kernel, tm=tm, rb=rb, tiles_per_seq=seq // tm)
    return pl.pallas_call(
        kern,
        grid=(T // tm, n_f),
        in_specs=[
            _resident((tm, D), lambda i, c: (i, 0)),
            pl.BlockSpec((HALO, D), lambda i, c: (jnp.maximum(i * (tm // HALO) - 1, 0), 0)),
            _resident((1, D), lambda i, c: (0, 0)),
            _resident((1, D), lambda i, c: (0, 0)),
            pl.BlockSpec((D, fc), lambda i, c: (0, c)),
            pl.BlockSpec((D, fc), lambda i, c: (0, c + n_f)),
            pl.BlockSpec((CONV_W, fc), lambda i, c: (0, c)),
            pl.BlockSpec((CONV_W, fc), lambda i, c: (0, c + n_f)),
            pl.BlockSpec((1, fc), lambda i, c: (0, c)),
            pl.BlockSpec((1, fc), lambda i, c: (0, c + n_f)),
            pl.BlockSpec((fc, D), lambda i, c: (c, 0)),
        ],
        out_specs=pl.BlockSpec((tm, D), lambda i, c: (i, 0)),
        out_shape=jax.ShapeDtypeStruct((T, D), _F32),
        scratch_shapes=[
            pltpu.VMEM((tm + HALO, D), _BF),
            pltpu.VMEM((tm // rb, rb + HALO, fc), _F32),
            pltpu.VMEM((tm // rb, rb + HALO, fc), _F32),
        ],
        compiler_params=_params(2),
        name="conv_ffn",
    )(r, r, g_pre, g_post, w_up, w_up, conv_w, conv_w, conv_b, conv_b, w_down)


def _ple_kernel(r_ref, p_ref, win_ref, wgate_ref, ge_ref, gr_ref, out_ref):
    r = r_ref[...]
    e = _rms(_dot(p_ref[...].astype(_BF), win_ref[...]), ge_ref[...])
    gate = _sigmoid(_dot(_rms(r, gr_ref[...]).astype(_BF), wgate_ref[...]))
    out_ref[...] = r + gate * e


def _ple_gate(r, p, w_in, w_gate, g_e, g_r):
    T, D = r.shape
    P = p.shape[1]
    tm = 512
    return pl.pallas_call(
        _ple_kernel,
        grid=(T // tm,),
        in_specs=[
            pl.BlockSpec((tm, D), lambda i: (i, 0)),
            pl.BlockSpec((tm, P), lambda i: (i, 0)),
            _resident((P, D), lambda i: (0, 0)),
            _resident((D, D), lambda i: (0, 0)),
            _resident((1, D), lambda i: (0, 0)),
            _resident((1, D), lambda i: (0, 0)),
        ],
        out_specs=pl.BlockSpec((tm, D), lambda i: (i, 0)),
        out_shape=jax.ShapeDtypeStruct((T, D), _F32),
        compiler_params=_params(1),
        name="ple_gate",
    )(r, p, w_in, w_gate, g_e, g_r)


def kernel(x, p, norm_g, hgrn_w_in, hgrn_lb_logits, hgrn_norm_g, hgrn_w_out, gmlp_w_in, gmlp_ln_g,
           gmlp_ln_b, gmlp_w_s, gmlp_b_s, gmlp_w_out, ffn_w_up, ffn_conv_w, ffn_conv_b, ffn_w_down,
           ple_w_in, ple_w_gate, ple_norm_g):
    B, S, D = x.shape
    depth = p.shape[0]
    H = D // LANES
    T = B * S
    row = lambda a: a.reshape(1, -1)

    lbl = hgrn_lb_logits.astype(_F32).reshape(-1, H, LANES).transpose(1, 0, 2)

    r = x.reshape(T, D)
    for i in range(depth):
        j = i // 2
        if i % 2 == 0:
            w_in = hgrn_w_in[j].astype(_BF).reshape(D, 4, H, LANES).transpose(2, 0, 1, 3)
            w_in = w_in.reshape(H, D, 4 * LANES)
            y = _hgrn_mixer(r, row(norm_g[i, 0]), w_in, lbl, row(hgrn_norm_g[j]),
                            layer=i, batch=B, seq=S)
            w_out = hgrn_w_out[j]
        else:
            y = _gmlp_mixer(r, row(norm_g[i, 0]), gmlp_w_in[j].astype(_BF), row(gmlp_ln_g[j]),
                            row(gmlp_ln_b[j]), gmlp_w_s[j], gmlp_b_s[j].T)
            w_out = gmlp_w_out[j]
        r = _proj_residual(y, w_out.astype(_BF), r, row(norm_g[i, 1]))
        r = _conv_ffn(r, row(norm_g[i, 2]), row(norm_g[i, 3]), ffn_w_up[i].astype(_BF),
                      ffn_conv_w[i], row(ffn_conv_b[i]), ffn_w_down[i].astype(_BF), seq=S)
        r = _ple_gate(r, p[i].reshape(T, -1), ple_w_in[i].astype(_BF), ple_w_gate[i].astype(_BF),
                      row(ple_norm_g[i, 0]), row(ple_norm_g[i, 1]))
    return r.reshape(B, S, D)
```

```python
import functools

import jax
import jax.numpy as jnp
from jax import lax
from jax.experimental import pallas as pl
from jax.experimental.pallas import tpu as pltpu

EPS = 1e-6
LANES = 128
HGRN_CHUNK = 64
HGRN_SAFE_DECAY = 60.0
GMLP_CHUNK = 128
CONV_W = 3
HALO = 16
NORM_ROWS = 128
VMEM_LIMIT = 60 * 1024 * 1024

_BF = jnp.bfloat16
_F32 = jnp.float32


def _params(n_axes):
    return pltpu.CompilerParams(
        dimension_semantics=("arbitrary",) * n_axes,
        vmem_limit_bytes=VMEM_LIMIT,
    )


def _resident(shape, index_map):
    return pl.BlockSpec(shape, index_map, pipeline_mode=pl.Buffered(1))


def _rms(x, g):
    return x * lax.rsqrt(jnp.mean(x * x, axis=-1, keepdims=True) + EPS) * g


def _sigmoid(x):
    return 1.0 / (1.0 + jnp.exp(-x))


def _dot(a, b):
    return jnp.dot(a, b, preferred_element_type=_F32)


def _dot_nt(a, b):
    return lax.dot_general(a, b, (((1,), (1,)), ((), ())), preferred_element_type=_F32)


def _dot_tn(a, b):
    return lax.dot_general(a, b, (((0,), (0,)), ((), ())), preferred_element_type=_F32)


def _chunk_cumsum(x, pos):
    shift = 1
    while shift < HGRN_CHUNK:
        x = x + jnp.where(pos >= shift, pltpu.roll(x, shift, axis=0), 0.0)
        shift *= 2
    return x


def _hgrn_gates(proj, lb):
    q = proj[:, 0:LANES]
    f = lb + (1.0 - lb) * _sigmoid(proj[:, LANES:2 * LANES])
    v = proj[:, 2 * LANES:3 * LANES]
    og = proj[:, 3 * LANES:4 * LANES]
    pos = lax.broadcasted_iota(jnp.int32, f.shape, 0) % HGRN_CHUNK
    return q, 1.0 - f, v, og, _chunk_cumsum(jnp.log(f), pos)


def _hgrn_out(o, og, ng):
    return (_rms(o, ng) * (og * _sigmoid(og))).astype(_BF)


def _hgrn_kernel(r_ref, gpre_ref, win_ref, lbl_ref, ng_ref, y_ref, hn_ref, st_ref, proj_ref, o_ref,
                 fq_ref, fb_ref, *, layer, tm, rb):
    n = pl.program_id(1)
    h = pl.program_id(2)

    @pl.when(h == 0)
    def _():
        def norm_rows(j, carry):
            rows = pl.ds(pl.multiple_of(j * NORM_ROWS, NORM_ROWS), NORM_ROWS)
            hn_ref[rows, :] = _rms(r_ref[rows, :], gpre_ref[...]).astype(_BF)
            return carry

        lax.fori_loop(0, tm // NORM_ROWS, norm_rows, 0)

    @pl.when(n == 0)
    def _():
        st_ref[h] = jnp.zeros((LANES, LANES), _F32)

    lg = lbl_ref[...]
    ex = jnp.exp(lg - jnp.max(lg, axis=0, keepdims=True))
    lb = jnp.sum(ex[0:layer + 1], axis=0, keepdims=True) / jnp.sum(ex, axis=0, keepdims=True)

    n_blocks = tm // rb
    for qb in range(n_blocks):
        rows = pl.ds(qb * rb, rb)
        proj_ref[rows, :] = _dot(hn_ref[rows, :], win_ref[...])

    row = lax.broadcasted_iota(jnp.int32, (HGRN_CHUNK, HGRN_CHUNK), 0)
    col = lax.broadcasted_iota(jnp.int32, (HGRN_CHUNK, HGRN_CHUNK), 1)
    causal = row >= col

    st = st_ref[h]
    any_unsafe = jnp.zeros((1, LANES), _F32)
    for qb in range(n_blocks):
        rows = pl.ds(qb * rb, rb)
        q, k, v, og, bcum = _hgrn_gates(proj_ref[rows, :], lb)
        outs = []
        for c in range(rb // HGRN_CHUNK):
            sl = slice(c * HGRN_CHUNK, (c + 1) * HGRN_CHUNK)
            b = bcum[sl]
            qc, kc, vc = q[sl], k[sl], v[sl]
            b_mid = b[HGRN_CHUNK // 2 - 1:HGRN_CHUNK // 2]
            b_last = b[HGRN_CHUNK - 1:HGRN_CHUNK]
            unsafe = b_last < -HGRN_SAFE_DECAY
            any_unsafe = jnp.maximum(any_unsafe, unsafe.astype(_F32))
            q_mid = jnp.where(unsafe, 0.0, qc * jnp.exp(b - b_mid)).astype(_BF)
            k_mid = jnp.where(unsafe, 0.0, kc * jnp.exp(b_mid - b)).astype(_BF)
            scores = jnp.where(causal, _dot_nt(q_mid, k_mid), 0.0)
            vb = vc.astype(_BF)
            o = _dot(scores.astype(_BF), vb) + _dot_nt((qc * jnp.exp(b)).astype(_BF), st.astype(_BF))
            k_dec = (kc * jnp.exp(b_last - b)).astype(_BF)
            st = st * jnp.exp(b_last) + _dot_tn(vb, k_dec)
            outs.append(o)
        o = jnp.concatenate(outs, axis=0)
        o_ref[rows, :] = o
        y_ref[rows, :] = _hgrn_out(o, og, ng_ref[...])
    st_ref[h] = st

    @pl.when(jnp.max(any_unsafe) > 0.0)
    def _():
        s_idx = lax.broadcasted_iota(jnp.int32, (HGRN_CHUNK, 1), 0)

        def chunk_body(c, carry):
            base = pl.multiple_of(c * HGRN_CHUNK, HGRN_CHUNK)
            q, k, v, og, b = _hgrn_gates(proj_ref[pl.ds(base, HGRN_CHUNK), :], lb)
            unsafe = b[HGRN_CHUNK - 1:HGRN_CHUNK] < -HGRN_SAFE_DECAY
            fq_ref[...] = jnp.where(unsafe, q, 0.0)
            fb_ref[...] = b

            def row_body(t, carry2):
                qt = fq_ref[pl.ds(t, 1), :]
                bt = fb_ref[pl.ds(t, 1), :]
                w = qt * k * jnp.exp(jnp.minimum(bt - b, 0.0))
                w = jnp.where(s_idx <= t, w, 0.0)
                sc = jnp.sum(w, axis=1, keepdims=True)
                o_ref[pl.ds(base + t, 1), :] += jnp.sum(sc * v, axis=0, keepdims=True)
                return carry2

            lax.fori_loop(0, HGRN_CHUNK, row_body, 0)
            y_ref[pl.ds(base, HGRN_CHUNK), :] = _hgrn_out(o_ref[pl.ds(base, HGRN_CHUNK), :], og, ng_ref[...])
            return carry

        lax.fori_loop(0, tm // HGRN_CHUNK, chunk_body, 0)


def _hgrn_mixer(r, g_pre, w_in_heads, lb_logits_heads, norm_g, *, layer, batch, seq):
    T, D = r.shape
    H = D // LANES
    tm = min(1024, seq)
    rb = min(256, tm)
    n_s = seq // tm
    kern = functools.partial(_hgrn_kernel, layer=layer, tm=tm, rb=rb)
    return pl.pallas_call(
        kern,
        grid=(batch, n_s, H),
        in_specs=[
            pl.BlockSpec((tm, D), lambda b, n, h: (b * n_s + n, 0)),
            _resident((1, D), lambda b, n, h: (0, 0)),
            pl.BlockSpec((None, D, 4 * LANES), lambda b, n, h: (h, 0, 0)),
            pl.BlockSpec((None,) + lb_logits_heads.shape[1:], lambda b, n, h: (h, 0, 0)),
            _resident((1, LANES), lambda b, n, h: (0, 0)),
        ],
        out_specs=pl.BlockSpec((tm, LANES), lambda b, n, h: (b * n_s + n, h)),
        out_shape=jax.ShapeDtypeStruct((T, D), _BF),
        scratch_shapes=[
            pltpu.VMEM((tm, D), _BF),
            pltpu.VMEM((H, LANES, LANES), _F32),
            pltpu.VMEM((tm, 4 * LANES), _F32),
            pltpu.VMEM((tm, LANES), _F32),
            pltpu.VMEM((HGRN_CHUNK, LANES), _F32),
            pltpu.VMEM((HGRN_CHUNK, LANES), _F32),
        ],
        compiler_params=_params(3),
        name="hgrn_mixer",
    )(r, g_pre, w_in_heads, lb_logits_heads, norm_g)


def _gelu_erf(x):
    return 0.5 * x * (1.0 + lax.erf(x * (2.0 ** -0.5)))


def _gmlp_kernel(r_ref, gpre_ref, win_ref, lng_ref, lnb_ref, ws_ref, bst_ref, y_ref, *, tm):
    D = r_ref.shape[1]
    hn = _rms(r_ref[...], gpre_ref[...]).astype(_BF)
    u = _gelu_erf(_dot(hn, win_ref[:, 0:D]))
    v = _gelu_erf(_dot(hn, win_ref[:, D:2 * D]))
    mu = jnp.mean(v, axis=-1, keepdims=True)
    vc = v - mu
    var = jnp.mean(vc * vc, axis=-1, keepdims=True)
    vb = (vc * lax.rsqrt(var + EPS) * lng_ref[...] + lnb_ref[...]).astype(_BF)

    row = lax.broadcasted_iota(jnp.int32, (GMLP_CHUNK, GMLP_CHUNK), 0)
    col = lax.broadcasted_iota(jnp.int32, (GMLP_CHUNK, GMLP_CHUNK), 1)
    causal = row >= col
    for g in range(D // LANES):
        cs = slice(g * LANES, (g + 1) * LANES)
        w = jnp.where(causal, ws_ref[g], 0.0).astype(_BF)
        bias = bst_ref[:, g:g + 1]
        for c in range(tm // GMLP_CHUNK):
            rs = slice(c * GMLP_CHUNK, (c + 1) * GMLP_CHUNK)
            mixed = _dot(w, vb[rs, cs]) + bias
            y_ref[rs, cs] = (u[rs, cs] * mixed).astype(_BF)


def _gmlp_mixer(r, g_pre, w_in, ln_g, ln_b, w_s, b_s_t):
    T, D = r.shape
    G = D // LANES
    tm = 256
    return pl.pallas_call(
        functools.partial(_gmlp_kernel, tm=tm),
        grid=(T // tm,),
        in_specs=[
            pl.BlockSpec((tm, D), lambda i: (i, 0)),
            _resident((1, D), lambda i: (0, 0)),
            _resident((D, 2 * D), lambda i: (0, 0)),
            _resident((1, D), lambda i: (0, 0)),
            _resident((1, D), lambda i: (0, 0)),
            _resident((G, GMLP_CHUNK, GMLP_CHUNK), lambda i: (0, 0, 0)),
            _resident((GMLP_CHUNK, G), lambda i: (0, 0)),
        ],
        out_specs=pl.BlockSpec((tm, D), lambda i: (i, 0)),
        out_shape=jax.ShapeDtypeStruct((T, D), _BF),
        compiler_params=_params(1),
        name="gmlp_mixer",
    )(r, g_pre, w_in, ln_g, ln_b, w_s, b_s_t)


def _proj_kernel(y_ref, w_ref, r_ref, g_ref, out_ref):
    out_ref[...] = r_ref[...] + _rms(_dot(y_ref[...], w_ref[...]), g_ref[...])


def _proj_residual(y, w, r, g_post):
    T, D = r.shape
    tm = 512
    return pl.pallas_call(
        _proj_kernel,
        grid=(T // tm,),
        in_specs=[
            pl.BlockSpec((tm, D), lambda i: (i, 0)),
            _resident((D, D), lambda i: (0, 0)),
            pl.BlockSpec((tm, D), lambda i: (i, 0)),
            _resident((1, D), lambda i: (0, 0)),
        ],
        out_specs=pl.BlockSpec((tm, D), lambda i: (i, 0)),
        out_shape=jax.ShapeDtypeStruct((T, D), _F32),
        compiler_params=_params(1),
        name="proj_residual",
    )(y, w, r, g_post)


def _gelu_tanh(x):
    return 0.5 * x * (1.0 + jnp.tanh((2.0 / jnp.pi) ** 0.5 * (x + 0.044715 * (x * x * x))))


def _ffn_kernel(r_ref, halo_ref, gpre_ref, gpost_ref, wup_ref, cp_ref, wd_ref, out_ref, hn_ref, h_ref,
                *, tm, rb, fc, tiles_per_seq):
    i = pl.program_id(0)
    c = pl.program_id(1)

    @pl.when(c == 0)
    def _():
        def norm_rows(j, carry):
            src = pl.ds(pl.multiple_of(j * NORM_ROWS, NORM_ROWS), NORM_ROWS)
            dst = pl.ds(pl.multiple_of(HALO + j * NORM_ROWS, HALO), NORM_ROWS)
            hn_ref[dst, :] = _rms(r_ref[src, :], gpre_ref[...]).astype(_BF)
            out_ref[src, :] = jnp.zeros((NORM_ROWS, out_ref.shape[1]), _F32)
            return carry

        lax.fori_loop(0, tm // NORM_ROWS, norm_rows, 0)
        halo = _rms(halo_ref[...], gpre_ref[...])
        halo = jnp.where(i % tiles_per_seq == 0, 0.0, halo)
        hn_ref[pl.ds(0, HALO), :] = halo.astype(_BF)

    n_blocks = tm // rb
    for q in range(n_blocks):
        if q == 0:
            h_ref[0] = _dot(hn_ref[pl.ds(0, HALO + rb), :], wup_ref[...])
        else:
            h_ref[q, pl.ds(HALO, rb), :] = _dot(hn_ref[pl.ds(HALO + q * rb, rb), :], wup_ref[...])
            h_ref[q, pl.ds(HALO - 8, 8), :] = h_ref[q - 1, pl.ds(HALO + rb - 8, 8), :]

    cp = cp_ref[c]
    for q in range(n_blocks):
        acc = cp[CONV_W:CONV_W + 1] + h_ref[q, pl.ds(HALO, rb), :] * cp[CONV_W - 1:CONV_W]
        for j in range(1, CONV_W):
            acc = acc + h_ref[q, pl.ds(HALO - j, rb), :] * cp[CONV_W - 1 - j:CONV_W - j]
        act = (_gelu_tanh(acc[:, 0:fc]) * acc[:, fc:2 * fc]).astype(_BF)
        out_ref[pl.ds(q * rb, rb), :] += _dot(act, wd_ref[...])

    @pl.when(c == pl.num_programs(1) - 1)
    def _():
        def finish_rows(j, carry):
            rows = pl.ds(pl.multiple_of(j * NORM_ROWS, NORM_ROWS), NORM_ROWS)
            out_ref[rows, :] = r_ref[rows, :] + _rms(out_ref[rows, :], gpost_ref[...])
            return carry

        lax.fori_loop(0, tm // NORM_ROWS, finish_rows, 0)


def _conv_ffn(r, g_pre, g_post, w_up, conv_w, conv_b, w_down, *, seq):
    T, D = r.shape
    F = w_down.shape[0]
    tm = min(1024, seq)
    rb = min(256, tm)
    fc = min(512, F)
    n_f = F // fc
    w_up = w_up.reshape(D, 2, n_f, fc).transpose(2, 0, 1, 3).reshape(n_f, D, 2 * fc)
    cp = jnp.concatenate([conv_w, conv_b], axis=0)
    cp = cp.reshape(CONV_W + 1, 2, n_f, fc).transpose(2, 0, 1, 3).reshape(n_f, CONV_W + 1, 2 * fc)
    kern = functools.partial(_ffn_kernel, tm=tm, rb=rb, fc=fc, tiles_per_seq=seq // tm)
    return pl.pallas_call(
        kern,
        grid=(T // tm, n_f),
        in_specs=[
            _resident((tm, D), lambda i, c: (i, 0)),
            pl.BlockSpec((HALO, D), lambda i, c: (jnp.maximum(i * (tm // HALO) - 1, 0), 0)),
            _resident((1, D), lambda i, c: (0, 0)),
            _resident((1, D), lambda i, c: (0, 0)),
            pl.BlockSpec((None, D, 2 * fc), lambda i, c: (c, 0, 0)),
            _resident((n_f, CONV_W + 1, 2 * fc), lambda i, c: (0, 0, 0)),
            pl.BlockSpec((fc, D), lambda i, c: (c, 0)),
        ],
        out_specs=pl.BlockSpec((tm, D), lambda i, c: (i, 0)),
        out_shape=jax.ShapeDtypeStruct((T, D), _F32),
        scratch_shapes=[
            pltpu.VMEM((tm + HALO, D), _BF),
            pltpu.VMEM((tm // rb, rb + HALO, 2 * fc), _F32),
        ],
        compiler_params=_params(2),
        name="conv_ffn",
    )(r, r, g_pre, g_post, w_up, cp, w_down)


def _ple_kernel(r_ref, p_ref, win_ref, wgate_ref, ge_ref, gr_ref, out_ref):
    r = r_ref[...]
    e = _rms(_dot(p_ref[...].astype(_BF), win_ref[...]), ge_ref[...])
    gate = _sigmoid(_dot(_rms(r, gr_ref[...]).astype(_BF), wgate_ref[...]))
    out_ref[...] = r + gate * e


def _ple_gate(r, p, w_in, w_gate, g_e, g_r):
    T, D = r.shape
    P = p.shape[1]
    tm = 512
    return pl.pallas_call(
        _ple_kernel,
        grid=(T // tm,),
        in_specs=[
            pl.BlockSpec((tm, D), lambda i: (i, 0)),
            pl.BlockSpec((tm, P), lambda i: (i, 0)),
            _resident((P, D), lambda i: (0, 0)),
            _resident((D, D), lambda i: (0, 0)),
            _resident((1, D), lambda i: (0, 0)),
            _resident((1, D), lambda i: (0, 0)),
        ],
        out_specs=pl.BlockSpec((tm, D), lambda i: (i, 0)),
        out_shape=jax.ShapeDtypeStruct((T, D), _F32),
        compiler_params=_params(1),
        name="ple_gate",
    )(r, p, w_in, w_gate, g_e, g_r)


def kernel(x, p, norm_g, hgrn_w_in, hgrn_lb_logits, hgrn_norm_g, hgrn_w_out, gmlp_w_in, gmlp_ln_g,
           gmlp_ln_b, gmlp_w_s, gmlp_b_s, gmlp_w_out, ffn_w_up, ffn_conv_w, ffn_conv_b, ffn_w_down,
           ple_w_in, ple_w_gate, ple_norm_g):
    B, S, D = x.shape
    depth = p.shape[0]
    H = D // LANES
    T = B * S
    row = lambda a: a.reshape(1, -1)

    lbl = hgrn_lb_logits.astype(_F32).reshape(-1, H, LANES).transpose(1, 0, 2)

    r = x.reshape(T, D)
    for i in range(depth):
        j = i // 2
        if i % 2 == 0:
            w_in = hgrn_w_in[j].astype(_BF).reshape(D, 4, H, LANES).transpose(2, 0, 1, 3)
            w_in = w_in.reshape(H, D, 4 * LANES)
            y = _hgrn_mixer(r, row(norm_g[i, 0]), w_in, lbl, row(hgrn_norm_g[j]),
                            layer=i, batch=B, seq=S)
            w_out = hgrn_w_out[j]
        else:
            y = _gmlp_mixer(r, row(norm_g[i, 0]), gmlp_w_in[j].astype(_BF), row(gmlp_ln_g[j]),
                            row(gmlp_ln_b[j]), gmlp_w_s[j], gmlp_b_s[j].T)
            w_out = gmlp_w_out[j]
        r = _proj_residual(y, w_out.astype(_BF), r, row(norm_g[i, 1]))
        r = _conv_ffn(r, row(norm_g[i, 2]), row(norm_g[i, 3]), ffn_w_up[i].astype(_BF),
                      ffn_conv_w[i], row(ffn_conv_b[i]), ffn_w_down[i].astype(_BF), seq=S)
        r = _ple_gate(r, p[i].reshape(T, -1), ple_w_in[i].astype(_BF), ple_w_gate[i].astype(_BF),
                      row(ple_norm_g[i, 0]), row(ple_norm_g[i, 1]))
    return r.reshape(B, S, D)
```

```python
import functools

import jax
import jax.numpy as jnp
from jax import lax
from jax.experimental import pallas as pl
from jax.experimental.pallas import tpu as pltpu

EPS = 1e-6
LANES = 128
HGRN_CHUNK = 64
HGRN_SAFE_DECAY = 60.0
GMLP_CHUNK = 128
CONV_W = 3
HALO = 16
NORM_ROWS = 128
VMEM_LIMIT = 60 * 1024 * 1024

_BF = jnp.bfloat16
_F32 = jnp.float32


def _params(n_axes):
    return pltpu.CompilerParams(
        dimension_semantics=("arbitrary",) * n_axes,
        vmem_limit_bytes=VMEM_LIMIT,
    )


def _resident(shape, index_map):
    return pl.BlockSpec(shape, index_map, pipeline_mode=pl.Buffered(1))


def _rms(x, g):
    return x * lax.rsqrt(jnp.mean(x * x, axis=-1, keepdims=True) + EPS) * g


def _sigmoid(x):
    return 1.0 / (1.0 + jnp.exp(-x))


def _dot(a, b):
    return jnp.dot(a, b, preferred_element_type=_F32)


def _dot_nt(a, b):
    return lax.dot_general(a, b, (((1,), (1,)), ((), ())), preferred_element_type=_F32)


def _dot_tn(a, b):
    return lax.dot_general(a, b, (((0,), (0,)), ((), ())), preferred_element_type=_F32)


def _chunk_cumsum(x, pos):
    shift = 1
    while shift < HGRN_CHUNK:
        x = x + jnp.where(pos >= shift, pltpu.roll(x, shift, axis=0), 0.0)
        shift *= 2
    return x


def _hgrn_gates(proj, lb):
    q = proj[:, 0:LANES]
    f = lb + (1.0 - lb) * _sigmoid(proj[:, LANES:2 * LANES])
    v = proj[:, 2 * LANES:3 * LANES]
    og = proj[:, 3 * LANES:4 * LANES]
    pos = lax.broadcasted_iota(jnp.int32, f.shape, 0) % HGRN_CHUNK
    return q, 1.0 - f, v, og, _chunk_cumsum(jnp.log(f), pos)


def _hgrn_out(o, og, ng):
    return (_rms(o, ng) * (og * _sigmoid(og))).astype(_BF)


def _hgrn_kernel(r_ref, gpre_ref, wq_ref, wf_ref, wi_ref, wg_ref, lbl_ref, ng_ref, *rest, layer, tm, rb, n_cast):
    cast_in, rest = rest[:n_cast], rest[n_cast:]
    y_ref, cast_out = rest[0], rest[1:1 + n_cast]
    hn_ref, st_ref, proj_ref, o_ref, fq_ref, fb_ref, win_ref = rest[1 + n_cast:]
    n = pl.program_id(1)
    h = pl.program_id(2)

    for src_ref, dst_ref in zip(cast_in, cast_out):
        dst_ref[...] = src_ref[...].astype(_BF)

    for g, w_ref in enumerate((wq_ref, wf_ref, wi_ref, wg_ref)):
        win_ref[:, g * LANES:(g + 1) * LANES] = w_ref[...].astype(_BF)

    @pl.when(h == 0)
    def _():
        def norm_rows(j, carry):
            rows = pl.ds(pl.multiple_of(j * NORM_ROWS, NORM_ROWS), NORM_ROWS)
            hn_ref[rows, :] = _rms(r_ref[rows, :], gpre_ref[...]).astype(_BF)
            return carry

        lax.fori_loop(0, tm // NORM_ROWS, norm_rows, 0)

    @pl.when(n == 0)
    def _():
        st_ref[h] = jnp.zeros((LANES, LANES), _F32)

    lg = lbl_ref[...]
    ex = jnp.exp(lg - jnp.max(lg, axis=0, keepdims=True))
    lb = jnp.sum(ex[0:layer + 1], axis=0, keepdims=True) / jnp.sum(ex, axis=0, keepdims=True)

    n_blocks = tm // rb
    for qb in range(n_blocks):
        rows = pl.ds(qb * rb, rb)
        proj_ref[rows, :] = _dot(hn_ref[rows, :], win_ref[...])

    row = lax.broadcasted_iota(jnp.int32, (HGRN_CHUNK, HGRN_CHUNK), 0)
    col = lax.broadcasted_iota(jnp.int32, (HGRN_CHUNK, HGRN_CHUNK), 1)
    causal = row >= col

    st = st_ref[h]
    any_unsafe = jnp.zeros((1, LANES), _F32)
    for qb in range(n_blocks):
        rows = pl.ds(qb * rb, rb)
        q, k, v, og, bcum = _hgrn_gates(proj_ref[rows, :], lb)
        outs = []
        for c in range(rb // HGRN_CHUNK):
            sl = slice(c * HGRN_CHUNK, (c + 1) * HGRN_CHUNK)
            b = bcum[sl]
            qc, kc, vc = q[sl], k[sl], v[sl]
            b_mid = b[HGRN_CHUNK // 2 - 1:HGRN_CHUNK // 2]
            b_last = b[HGRN_CHUNK - 1:HGRN_CHUNK]
            unsafe = b_last < -HGRN_SAFE_DECAY
            any_unsafe = jnp.maximum(any_unsafe, unsafe.astype(_F32))
            q_mid = jnp.where(unsafe, 0.0, qc * jnp.exp(b - b_mid)).astype(_BF)
            k_mid = jnp.where(unsafe, 0.0, kc * jnp.exp(b_mid - b)).astype(_BF)
            scores = jnp.where(causal, _dot_nt(q_mid, k_mid), 0.0)
            vb = vc.astype(_BF)
            o = _dot(scores.astype(_BF), vb) + _dot_nt((qc * jnp.exp(b)).astype(_BF), st.astype(_BF))
            k_dec = (kc * jnp.exp(b_last - b)).astype(_BF)
            st = st * jnp.exp(b_last) + _dot_tn(vb, k_dec)
            outs.append(o)
        o = jnp.concatenate(outs, axis=0)
        o_ref[rows, :] = o
        y_ref[rows, :] = _hgrn_out(o, og, ng_ref[...])
    st_ref[h] = st

    @pl.when(jnp.max(any_unsafe) > 0.0)
    def _():
        s_idx = lax.broadcasted_iota(jnp.int32, (HGRN_CHUNK, 1), 0)

        def chunk_body(c, carry):
            base = pl.multiple_of(c * HGRN_CHUNK, HGRN_CHUNK)
            q, k, v, og, b = _hgrn_gates(proj_ref[pl.ds(base, HGRN_CHUNK), :], lb)
            unsafe = b[HGRN_CHUNK - 1:HGRN_CHUNK] < -HGRN_SAFE_DECAY
            fq_ref[...] = jnp.where(unsafe, q, 0.0)
            fb_ref[...] = b

            def row_body(t, carry2):
                qt = fq_ref[pl.ds(t, 1), :]
                bt = fb_ref[pl.ds(t, 1), :]
                w = qt * k * jnp.exp(jnp.minimum(bt - b, 0.0))
                w = jnp.where(s_idx <= t, w, 0.0)
                sc = jnp.sum(w, axis=1, keepdims=True)
                o_ref[pl.ds(base + t, 1), :] += jnp.sum(sc * v, axis=0, keepdims=True)
                return carry2

            lax.fori_loop(0, HGRN_CHUNK, row_body, 0)
            y_ref[pl.ds(base, HGRN_CHUNK), :] = _hgrn_out(o_ref[pl.ds(base, HGRN_CHUNK), :], og, ng_ref[...])
            return carry

        lax.fori_loop(0, tm // HGRN_CHUNK, chunk_body, 0)


def _hgrn_mixer(r, g_pre, w_in, lb_logits_heads, norm_g, to_cast, *, layer, batch, seq):
    T, D = r.shape
    H = D // LANES
    tm = min(1024, seq)
    rb = min(256, tm)
    n_s = seq // tm
    n_steps = batch * n_s * H
    step = lambda b, n, h: (b * n_s + n) * H + h
    cast_specs = []
    for a in to_cast:
        rows = a.shape[0] // n_steps
        assert rows * n_steps == a.shape[0] and rows % 16 == 0, a.shape
        cast_specs.append(pl.BlockSpec((rows, a.shape[1]), lambda b, n, h: (step(b, n, h), 0)))
    w_slab = lambda g: pl.BlockSpec((D, LANES), lambda b, n, h: (0, g * H + h))
    kern = functools.partial(_hgrn_kernel, layer=layer, tm=tm, rb=rb, n_cast=len(to_cast))
    out = pl.pallas_call(
        kern,
        grid=(batch, n_s, H),
        in_specs=[
            pl.BlockSpec((tm, D), lambda b, n, h: (b * n_s + n, 0)),
            _resident((1, D), lambda b, n, h: (0, 0)),
            w_slab(0), w_slab(1), w_slab(2), w_slab(3),
            pl.BlockSpec((None,) + lb_logits_heads.shape[1:], lambda b, n, h: (h, 0, 0)),
            _resident((1, LANES), lambda b, n, h: (0, 0)),
        ] + cast_specs,
        out_specs=[pl.BlockSpec((tm, LANES), lambda b, n, h: (b * n_s + n, h))] + cast_specs,
        out_shape=[jax.ShapeDtypeStruct((T, D), _BF)]
        + [jax.ShapeDtypeStruct(a.shape, _BF) for a in to_cast],
        scratch_shapes=[
            pltpu.VMEM((tm, D), _BF),
            pltpu.VMEM((H, LANES, LANES), _F32),
            pltpu.VMEM((tm, 4 * LANES), _F32),
            pltpu.VMEM((tm, LANES), _F32),
            pltpu.VMEM((HGRN_CHUNK, LANES), _F32),
            pltpu.VMEM((HGRN_CHUNK, LANES), _F32),
            pltpu.VMEM((D, 4 * LANES), _BF),
        ],
        compiler_params=_params(3),
        name="hgrn_mixer",
    )(r, g_pre, w_in, w_in, w_in, w_in, lb_logits_heads, norm_g, *to_cast)
    return out[0], out[1:]


def _gelu_erf(x):
    return 0.5 * x * (1.0 + lax.erf(x * (2.0 ** -0.5)))


def _gmlp_kernel(r_ref, gpre_ref, win_ref, lng_ref, lnb_ref, ws_ref, bst_ref, y_ref, *, tm):
    D = r_ref.shape[1]
    hn = _rms(r_ref[...], gpre_ref[...]).astype(_BF)
    u = _gelu_erf(_dot(hn, win_ref[:, 0:D]))
    v = _gelu_erf(_dot(hn, win_ref[:, D:2 * D]))
    mu = jnp.mean(v, axis=-1, keepdims=True)
    vc = v - mu
    var = jnp.mean(vc * vc, axis=-1, keepdims=True)
    vb = (vc * lax.rsqrt(var + EPS) * lng_ref[...] + lnb_ref[...]).astype(_BF)

    row = lax.broadcasted_iota(jnp.int32, (GMLP_CHUNK, GMLP_CHUNK), 0)
    col = lax.broadcasted_iota(jnp.int32, (GMLP_CHUNK, GMLP_CHUNK), 1)
    causal = row >= col
    for g in range(D // LANES):
        cs = slice(g * LANES, (g + 1) * LANES)
        w = jnp.where(causal, ws_ref[g], 0.0).astype(_BF)
        bias = bst_ref[:, g:g + 1]
        for c in range(tm // GMLP_CHUNK):
            rs = slice(c * GMLP_CHUNK, (c + 1) * GMLP_CHUNK)
            mixed = _dot(w, vb[rs, cs]) + bias
            y_ref[rs, cs] = (u[rs, cs] * mixed).astype(_BF)


def _gmlp_mixer(r, g_pre, w_in, ln_g, ln_b, w_s, b_s_t):
    T, D = r.shape
    G = D // LANES
    tm = 256
    return pl.pallas_call(
        functools.partial(_gmlp_kernel, tm=tm),
        grid=(T // tm,),
        in_specs=[
            pl.BlockSpec((tm, D), lambda i: (i, 0)),
            _resident((1, D), lambda i: (0, 0)),
            _resident((D, 2 * D), lambda i: (0, 0)),
            _resident((1, D), lambda i: (0, 0)),
            _resident((1, D), lambda i: (0, 0)),
            _resident((G, GMLP_CHUNK, GMLP_CHUNK), lambda i: (0, 0, 0)),
            _resident((GMLP_CHUNK, G), lambda i: (0, 0)),
        ],
        out_specs=pl.BlockSpec((tm, D), lambda i: (i, 0)),
        out_shape=jax.ShapeDtypeStruct((T, D), _BF),
        compiler_params=_params(1),
        name="gmlp_mixer",
    )(r, g_pre, w_in, ln_g, ln_b, w_s, b_s_t)


def _proj_kernel(y_ref, w_ref, r_ref, g_ref, out_ref):
    out_ref[...] = r_ref[...] + _rms(_dot(y_ref[...], w_ref[...]), g_ref[...])


def _proj_residual(y, w, r, g_post):
    T, D = r.shape
    tm = 512
    return pl.pallas_call(
        _proj_kernel,
        grid=(T // tm,),
        in_specs=[
            pl.BlockSpec((tm, D), lambda i: (i, 0)),
            _resident((D, D), lambda i: (0, 0)),
            pl.BlockSpec((tm, D), lambda i: (i, 0)),
            _resident((1, D), lambda i: (0, 0)),
        ],
        out_specs=pl.BlockSpec((tm, D), lambda i: (i, 0)),
        out_shape=jax.ShapeDtypeStruct((T, D), _F32),
        compiler_params=_params(1),
        name="proj_residual",
    )(y, w, r, g_post)


def _gelu_tanh(x):
    return 0.5 * x * (1.0 + jnp.tanh((2.0 / jnp.pi) ** 0.5 * (x + 0.044715 * (x * x * x))))


def _ffn_kernel(r_ref, halo_ref, gpre_ref, gpost_ref, wg_ref, wv_ref, cwg_ref, cwv_ref,
                cbg_ref, cbv_ref, wd_ref, out_ref, hn_ref, hg_ref, hv_ref, *, tm, rb, tiles_per_seq):
    i = pl.program_id(0)
    c = pl.program_id(1)

    @pl.when(c == 0)
    def _():
        hn_ref[pl.ds(HALO, tm), :] = _rms(r_ref[...], gpre_ref[...]).astype(_BF)
        halo = _rms(halo_ref[...], gpre_ref[...])
        halo = jnp.where(i % tiles_per_seq == 0, 0.0, halo)
        hn_ref[pl.ds(0, HALO), :] = halo.astype(_BF)
        out_ref[...] = jnp.zeros(out_ref.shape, _F32)

    n_blocks = tm // rb
    for q in range(n_blocks):
        for h_ref, w_ref in ((hg_ref, wg_ref), (hv_ref, wv_ref)):
            if q == 0:
                h_ref[0] = _dot(hn_ref[pl.ds(0, HALO + rb), :], w_ref[...])
            else:
                h_ref[q, pl.ds(HALO, rb), :] = _dot(hn_ref[pl.ds(HALO + q * rb, rb), :], w_ref[...])
                h_ref[q, pl.ds(HALO - 8, 8), :] = h_ref[q - 1, pl.ds(HALO + rb - 8, 8), :]

    def conv(h_ref, q, cw_ref, cb_ref):
        acc = cb_ref[...] + h_ref[q, pl.ds(HALO, rb), :] * cw_ref[CONV_W - 1:CONV_W, :]
        for j in range(1, CONV_W):
            acc = acc + h_ref[q, pl.ds(HALO - j, rb), :] * cw_ref[CONV_W - 1 - j:CONV_W - j, :]
        return acc

    for q in range(n_blocks):
        gate = conv(hg_ref, q, cwg_ref, cbg_ref)
        val = conv(hv_ref, q, cwv_ref, cbv_ref)
        act = (_gelu_tanh(gate) * val).astype(_BF)
        out_ref[pl.ds(q * rb, rb), :] += _dot(act, wd_ref[...])

    @pl.when(c == pl.num_programs(1) - 1)
    def _():
        out_ref[...] = r_ref[...] + _rms(out_ref[...], gpost_ref[...])


def _conv_ffn(r, g_pre, g_post, w_up, conv_w, conv_b, w_down, *, layer, seq):
    T, D = r.shape
    F = w_up.shape[1] // 2
    tm = min(1024, seq)
    rb = min(256, tm)
    fc = min(512, F)
    n_f = F // fc
    kern = functools.partial(_ffn_kernel, tm=tm, rb=rb, tiles_per_seq=seq // tm)
    return pl.pallas_call(
        kern,
        grid=(T // tm, n_f),
        in_specs=[
            _resident((tm, D), lambda i, c: (i, 0)),
            pl.BlockSpec((HALO, D), lambda i, c: (jnp.maximum(i * (tm // HALO) - 1, 0), 0)),
            _resident((1, D), lambda i, c: (0, 0)),
            _resident((1, D), lambda i, c: (0, 0)),
            pl.BlockSpec((D, fc), lambda i, c: (layer, c)),
            pl.BlockSpec((D, fc), lambda i, c: (layer, c + n_f)),
            pl.BlockSpec((CONV_W, fc), lambda i, c: (0, c)),
            pl.BlockSpec((CONV_W, fc), lambda i, c: (0, c + n_f)),
            pl.BlockSpec((1, fc), lambda i, c: (0, c)),
            pl.BlockSpec((1, fc), lambda i, c: (0, c + n_f)),
            pl.BlockSpec((fc, D), lambda i, c: (layer * n_f + c, 0)),
        ],
        out_specs=pl.BlockSpec((tm, D), lambda i, c: (i, 0)),
        out_shape=jax.ShapeDtypeStruct((T, D), _F32),
        scratch_shapes=[
            pltpu.VMEM((tm + HALO, D), _BF),
            pltpu.VMEM((tm // rb, rb + HALO, fc), _F32),
            pltpu.VMEM((tm // rb, rb + HALO, fc), _F32),
        ],
        compiler_params=_params(2),
        name="conv_ffn",
    )(r, r, g_pre, g_post, w_up, w_up, conv_w, conv_w, conv_b, conv_b, w_down)


def _ple_kernel(r_ref, p_ref, win_ref, wgate_ref, ge_ref, gr_ref, out_ref):
    r = r_ref[...]
    e = _rms(_dot(p_ref[...].astype(_BF), win_ref[...]), ge_ref[...])
    gate = _sigmoid(_dot(_rms(r, gr_ref[...]).astype(_BF), wgate_ref[...]))
    out_ref[...] = r + gate * e


def _ple_gate(r, p, w_in, w_gate, g_e, g_r):
    T, D = r.shape
    P = p.shape[1]
    tm = 512
    return pl.pallas_call(
        _ple_kernel,
        grid=(T // tm,),
        in_specs=[
            pl.BlockSpec((tm, D), lambda i: (i, 0)),
            pl.BlockSpec((tm, P), lambda i: (i, 0)),
            _resident((P, D), lambda i: (0, 0)),
            _resident((D, D), lambda i: (0, 0)),
            _resident((1, D), lambda i: (0, 0)),
            _resident((1, D), lambda i: (0, 0)),
        ],
        out_specs=pl.BlockSpec((tm, D), lambda i: (i, 0)),
        out_shape=jax.ShapeDtypeStruct((T, D), _F32),
        compiler_params=_params(1),
        name="ple_gate",
    )(r, p, w_in, w_gate, g_e, g_r)


def kernel(x, p, norm_g, hgrn_w_in, hgrn_lb_logits, hgrn_norm_g, hgrn_w_out, gmlp_w_in, gmlp_ln_g,
           gmlp_ln_b, gmlp_w_s, gmlp_b_s, gmlp_w_out, ffn_w_up, ffn_conv_w, ffn_conv_b, ffn_w_down,
           ple_w_in, ple_w_gate, ple_norm_g):
    B, S, D = x.shape
    depth = p.shape[0]
    H = D // LANES
    T = B * S
    row = lambda a: a.reshape(1, -1)

    lbl = hgrn_lb_logits.astype(_F32).reshape(-1, H, LANES).transpose(1, 0, 2)

    F = ffn_w_down.shape[1]
    ffn_up = ffn_w_up.reshape(depth * D, 2 * F)
    ffn_down = ffn_w_down.reshape(depth * F, D)
    ffn_up_bf = ffn_down_bf = None

    r = x.reshape(T, D)
    for i in range(depth):
        j = i // 2
        if i % 2 == 0:
            to_cast = (ffn_up, ffn_down) if ffn_up_bf is None else ()
            y, cast = _hgrn_mixer(r, row(norm_g[i, 0]), hgrn_w_in[j], lbl, row(hgrn_norm_g[j]), to_cast,
                                  layer=i, batch=B, seq=S)
            if to_cast:
                ffn_up_bf, ffn_down_bf = cast
            w_out = hgrn_w_out[j]
        else:
            y = _gmlp_mixer(r, row(norm_g[i, 0]), gmlp_w_in[j].astype(_BF), row(gmlp_ln_g[j]),
                            row(gmlp_ln_b[j]), gmlp_w_s[j], gmlp_b_s[j].T)
            w_out = gmlp_w_out[j]
        r = _proj_residual(y, w_out.astype(_BF), r, row(norm_g[i, 1]))
        r = _conv_ffn(r, row(norm_g[i, 2]), row(norm_g[i, 3]), ffn_up_bf, ffn_conv_w[i],
                      row(ffn_conv_b[i]), ffn_down_bf, layer=i, seq=S)
        r = _ple_gate(r, p[i].reshape(T, -1), ple_w_in[i].astype(_BF), ple_w_gate[i].astype(_BF),
                      row(ple_norm_g[i, 0]), row(ple_norm_g[i, 1]))
    return r.reshape(B, S, D)
```

```python
import functools

import jax
import jax.numpy as jnp
from jax import lax
from jax.experimental import pallas as pl
from jax.experimental.pallas import tpu as pltpu

EPS = 1e-6
LANES = 128
HGRN_CHUNK = 64
HGRN_SAFE_DECAY = 60.0
HGRN_HEADS_PER_STEP = 2
GMLP_CHUNK = 128
CONV_W = 3
HALO = 16
NORM_ROWS = 128
VMEM_LIMIT = 60 * 1024 * 1024

_BF = jnp.bfloat16
_F32 = jnp.float32


def _params(n_axes):
    return pltpu.CompilerParams(
        dimension_semantics=("arbitrary",) * n_axes,
        vmem_limit_bytes=VMEM_LIMIT,
    )


def _resident(shape, index_map):
    return pl.BlockSpec(shape, index_map, pipeline_mode=pl.Buffered(1))


def _rms(x, g):
    return x * lax.rsqrt(jnp.mean(x * x, axis=-1, keepdims=True) + EPS) * g


def _sigmoid(x):
    return 1.0 / (1.0 + jnp.exp(-x))


def _dot(a, b):
    return jnp.dot(a, b, preferred_element_type=_F32)


def _dot_nt(a, b):
    return lax.dot_general(a, b, (((1,), (1,)), ((), ())), preferred_element_type=_F32)


def _dot_tn(a, b):
    return lax.dot_general(a, b, (((0,), (0,)), ((), ())), preferred_element_type=_F32)


def _chunk_cumsum(x, pos):
    shift = 1
    while shift < HGRN_CHUNK:
        x = x + jnp.where(pos >= shift, pltpu.roll(x, shift, axis=0), 0.0)
        shift *= 2
    return x


def _hgrn_gates(proj, lb):
    q = proj[:, 0:LANES]
    f = lb + (1.0 - lb) * _sigmoid(proj[:, LANES:2 * LANES])
    v = proj[:, 2 * LANES:3 * LANES]
    og = proj[:, 3 * LANES:4 * LANES]
    pos = lax.broadcasted_iota(jnp.int32, f.shape, 0) % HGRN_CHUNK
    return q, 1.0 - f, v, og, _chunk_cumsum(jnp.log(f), pos)


def _hgrn_out(o, og, ng):
    return (_rms(o, ng) * (og * _sigmoid(og))).astype(_BF)


def _hgrn_wcast_kernel(*refs):
    out_ref = refs[-1]
    for s, w_ref in enumerate(refs[:-1]):
        out_ref[:, s * LANES:(s + 1) * LANES] = w_ref[...].astype(_BF)


def _hgrn_weights(w_in):
    D = w_in.shape[0]
    H = D // LANES
    hp = HGRN_HEADS_PER_STEP
    specs = [pl.BlockSpec((D, LANES), lambda p, hh=hh, g=g: (0, g * H + p * hp + hh))
             for hh in range(hp) for g in range(4)]
    return pl.pallas_call(
        _hgrn_wcast_kernel,
        grid=(H // hp,),
        in_specs=specs,
        out_specs=pl.BlockSpec((None, D, hp * 4 * LANES), lambda p: (p, 0, 0)),
        out_shape=jax.ShapeDtypeStruct((H // hp, D, hp * 4 * LANES), _BF),
        compiler_params=_params(1),
        name="hgrn_weights",
    )(*([w_in] * (4 * hp)))


def _hgrn_kernel(r_ref, gpre_ref, win_ref, lbl_ref, ng_ref, *rest, layer, tm, rb, n_cast):
    cast_in, rest = rest[:n_cast], rest[n_cast:]
    y_ref, cast_out = rest[0], rest[1:1 + n_cast]
    hn_ref, st_ref, proj_ref, o_ref, fq_ref, fb_ref = rest[1 + n_cast:]
    n = pl.program_id(1)
    p = pl.program_id(2)
    heads = range(HGRN_HEADS_PER_STEP)
    head_cols = lambda hh, w: slice(hh * w, (hh + 1) * w)

    for src_ref, dst_ref in zip(cast_in, cast_out):
        dst_ref[...] = src_ref[...].astype(_BF)

    @pl.when(p == 0)
    def _():
        def norm_rows(j, carry):
            rows = pl.ds(pl.multiple_of(j * NORM_ROWS, NORM_ROWS), NORM_ROWS)
            hn_ref[rows, :] = _rms(r_ref[rows, :], gpre_ref[...]).astype(_BF)
            return carry

        lax.fori_loop(0, tm // NORM_ROWS, norm_rows, 0)

    @pl.when(n == 0)
    def _():
        for hh in heads:
            st_ref[p * HGRN_HEADS_PER_STEP + hh] = jnp.zeros((LANES, LANES), _F32)

    lbs = []
    for hh in heads:
        lg = lbl_ref[hh]
        ex = jnp.exp(lg - jnp.max(lg, axis=0, keepdims=True))
        lbs.append(jnp.sum(ex[0:layer + 1], axis=0, keepdims=True) / jnp.sum(ex, axis=0, keepdims=True))

    n_blocks = tm // rb
    for qb in range(n_blocks):
        rows = pl.ds(qb * rb, rb)
        proj_ref[rows, :] = _dot(hn_ref[rows, :], win_ref[...])

    row = lax.broadcasted_iota(jnp.int32, (HGRN_CHUNK, HGRN_CHUNK), 0)
    col = lax.broadcasted_iota(jnp.int32, (HGRN_CHUNK, HGRN_CHUNK), 1)
    causal = row >= col

    sts = [st_ref[p * HGRN_HEADS_PER_STEP + hh] for hh in heads]
    any_unsafe = jnp.zeros((1, LANES), _F32)
    for qb in range(n_blocks):
        rows = pl.ds(qb * rb, rb)
        gates = [_hgrn_gates(proj_ref[rows, head_cols(hh, 4 * LANES)], lbs[hh]) for hh in heads]
        pairs = [(c, hh) for c in range(rb // HGRN_CHUNK) for hh in heads]
        q_mid, k_mid, q_dec, k_dec, vb, d_last = {}, {}, {}, {}, {}, {}
        for c, hh in pairs:
            sl = slice(c * HGRN_CHUNK, (c + 1) * HGRN_CHUNK)
            q, k, v, _, bcum = gates[hh]
            b = bcum[sl]
            qc, kc = q[sl], k[sl]
            b_mid = b[HGRN_CHUNK // 2 - 1:HGRN_CHUNK // 2]
            b_last = b[HGRN_CHUNK - 1:HGRN_CHUNK]
            unsafe = b_last < -HGRN_SAFE_DECAY
            any_unsafe = jnp.maximum(any_unsafe, unsafe.astype(_F32))
            q_mid[c, hh] = jnp.where(unsafe, 0.0, qc * jnp.exp(b - b_mid)).astype(_BF)
            k_mid[c, hh] = jnp.where(unsafe, 0.0, kc * jnp.exp(b_mid - b)).astype(_BF)
            q_dec[c, hh] = (qc * jnp.exp(b)).astype(_BF)
            k_dec[c, hh] = (kc * jnp.exp(b_last - b)).astype(_BF)
            vb[c, hh] = v[sl].astype(_BF)
            d_last[c, hh] = jnp.exp(b_last)
        scores = {ph: _dot_nt(q_mid[ph], k_mid[ph]) for ph in pairs}
        st_add = {ph: _dot_tn(vb[ph], k_dec[ph]) for ph in pairs}
        o_intra = {ph: _dot(jnp.where(causal, scores[ph], 0.0).astype(_BF), vb[ph]) for ph in pairs}
        st_in = {}
        for c, hh in pairs:
            st_in[c, hh] = sts[hh]
            sts[hh] = sts[hh] * d_last[c, hh] + st_add[c, hh]
        o_inter = {ph: _dot_nt(q_dec[ph], st_in[ph].astype(_BF)) for ph in pairs}
        for hh in heads:
            o = jnp.concatenate([o_intra[c, hh] + o_inter[c, hh] for c in range(rb // HGRN_CHUNK)], axis=0)
            o_ref[hh, rows, :] = o
            y_ref[rows, head_cols(hh, LANES)] = _hgrn_out(o, gates[hh][3], ng_ref[...])
    for hh in heads:
        st_ref[p * HGRN_HEADS_PER_STEP + hh] = sts[hh]

    @pl.when(jnp.max(any_unsafe) > 0.0)
    def _():
        s_idx = lax.broadcasted_iota(jnp.int32, (HGRN_CHUNK, 1), 0)

        for hh in heads:
            def chunk_body(c, carry, hh=hh):
                base = pl.multiple_of(c * HGRN_CHUNK, HGRN_CHUNK)
                chunk = pl.ds(base, HGRN_CHUNK)
                q, k, v, og, b = _hgrn_gates(proj_ref[chunk, head_cols(hh, 4 * LANES)], lbs[hh])
                unsafe = b[HGRN_CHUNK - 1:HGRN_CHUNK] < -HGRN_SAFE_DECAY
                fq_ref[...] = jnp.where(unsafe, q, 0.0)
                fb_ref[...] = b

                def row_body(t, carry2):
                    qt = fq_ref[pl.ds(t, 1), :]
                    bt = fb_ref[pl.ds(t, 1), :]
                    w = qt * k * jnp.exp(jnp.minimum(bt - b, 0.0))
                    w = jnp.where(s_idx <= t, w, 0.0)
                    sc = jnp.sum(w, axis=1, keepdims=True)
                    o_ref[hh, pl.ds(base + t, 1), :] += jnp.sum(sc * v, axis=0, keepdims=True)
                    return carry2

                lax.fori_loop(0, HGRN_CHUNK, row_body, 0)
                y_ref[chunk, head_cols(hh, LANES)] = _hgrn_out(o_ref[hh, chunk, :], og, ng_ref[...])
                return carry

            lax.fori_loop(0, tm // HGRN_CHUNK, chunk_body, 0)


def _hgrn_mixer(r, g_pre, w_in_groups, lb_logits_heads, norm_g, to_cast, *, layer, batch, seq):
    T, D = r.shape
    H = D // LANES
    hp = HGRN_HEADS_PER_STEP
    tm = min(1024, seq)
    rb = min(256, tm)
    n_s = seq // tm
    n_steps = batch * n_s * (H // hp)
    step = lambda b, n, p: (b * n_s + n) * (H // hp) + p
    cast_specs = []
    for a in to_cast:
        rows = a.shape[0] // n_steps
        assert rows * n_steps == a.shape[0] and rows % 16 == 0, a.shape
        cast_specs.append(pl.BlockSpec((rows, a.shape[1]), lambda b, n, p: (step(b, n, p), 0)))
    kern = functools.partial(_hgrn_kernel, layer=layer, tm=tm, rb=rb, n_cast=len(to_cast))
    out = pl.pallas_call(
        kern,
        grid=(batch, n_s, H // hp),
        in_specs=[
            pl.BlockSpec((tm, D), lambda b, n, p: (b * n_s + n, 0)),
            _resident((1, D), lambda b, n, p: (0, 0)),
            pl.BlockSpec((None, D, hp * 4 * LANES), lambda b, n, p: (p, 0, 0)),
            pl.BlockSpec((hp,) + lb_logits_heads.shape[1:], lambda b, n, p: (p, 0, 0)),
            _resident((1, LANES), lambda b, n, p: (0, 0)),
        ] + cast_specs,
        out_specs=[pl.BlockSpec((tm, hp * LANES), lambda b, n, p: (b * n_s + n, p))] + cast_specs,
        out_shape=[jax.ShapeDtypeStruct((T, D), _BF)]
        + [jax.ShapeDtypeStruct(a.shape, _BF) for a in to_cast],
        scratch_shapes=[
            pltpu.VMEM((tm, D), _BF),
            pltpu.VMEM((H, LANES, LANES), _F32),
            pltpu.VMEM((tm, hp * 4 * LANES), _F32),
            pltpu.VMEM((hp, tm, LANES), _F32),
            pltpu.VMEM((HGRN_CHUNK, LANES), _F32),
            pltpu.VMEM((HGRN_CHUNK, LANES), _F32),
        ],
        compiler_params=_params(3),
        name="hgrn_mixer",
    )(r, g_pre, w_in_groups, lb_logits_heads, norm_g, *to_cast)
    return out[0], out[1:]


def _gelu_erf(x):
    return 0.5 * x * (1.0 + lax.erf(x * (2.0 ** -0.5)))


def _gmlp_kernel(r_ref, gpre_ref, win_ref, lng_ref, lnb_ref, ws_ref, bst_ref, y_ref, *, tm):
    D = r_ref.shape[1]
    hn = _rms(r_ref[...], gpre_ref[...]).astype(_BF)
    u = _gelu_erf(_dot(hn, win_ref[:, 0:D]))
    v = _gelu_erf(_dot(hn, win_ref[:, D:2 * D]))
    mu = jnp.mean(v, axis=-1, keepdims=True)
    vc = v - mu
    var = jnp.mean(vc * vc, axis=-1, keepdims=True)
    vb = (vc * lax.rsqrt(var + EPS) * lng_ref[...] + lnb_ref[...]).astype(_BF)

    row = lax.broadcasted_iota(jnp.int32, (GMLP_CHUNK, GMLP_CHUNK), 0)
    col = lax.broadcasted_iota(jnp.int32, (GMLP_CHUNK, GMLP_CHUNK), 1)
    causal = row >= col
    for g in range(D // LANES):
        cs = slice(g * LANES, (g + 1) * LANES)
        w = jnp.where(causal, ws_ref[g], 0.0).astype(_BF)
        bias = bst_ref[:, g:g + 1]
        for c in range(tm // GMLP_CHUNK):
            rs = slice(c * GMLP_CHUNK, (c + 1) * GMLP_CHUNK)
            mixed = _dot(w, vb[rs, cs]) + bias
            y_ref[rs, cs] = (u[rs, cs] * mixed).astype(_BF)


def _gmlp_mixer(r, g_pre, w_in, ln_g, ln_b, w_s, b_s_t):
    T, D = r.shape
    G = D // LANES
    tm = 256
    return pl.pallas_call(
        functools.partial(_gmlp_kernel, tm=tm),
        grid=(T // tm,),
        in_specs=[
            pl.BlockSpec((tm, D), lambda i: (i, 0)),
            _resident((1, D), lambda i: (0, 0)),
            _resident((D, 2 * D), lambda i: (0, 0)),
            _resident((1, D), lambda i: (0, 0)),
            _resident((1, D), lambda i: (0, 0)),
            _resident((G, GMLP_CHUNK, GMLP_CHUNK), lambda i: (0, 0, 0)),
            _resident((GMLP_CHUNK, G), lambda i: (0, 0)),
        ],
        out_specs=pl.BlockSpec((tm, D), lambda i: (i, 0)),
        out_shape=jax.ShapeDtypeStruct((T, D), _BF),
        compiler_params=_params(1),
        name="gmlp_mixer",
    )(r, g_pre, w_in, ln_g, ln_b, w_s, b_s_t)


def _proj_kernel(y_ref, w_ref, r_ref, g_ref, out_ref):
    out_ref[...] = r_ref[...] + _rms(_dot(y_ref[...], w_ref[...]), g_ref[...])


def _proj_residual(y, w, r, g_post):
    T, D = r.shape
    tm = 512
    return pl.pallas_call(
        _proj_kernel,
        grid=(T // tm,),
        in_specs=[
            pl.BlockSpec((tm, D), lambda i: (i, 0)),
            _resident((D, D), lambda i: (0, 0)),
            pl.BlockSpec((tm, D), lambda i: (i, 0)),
            _resident((1, D), lambda i: (0, 0)),
        ],
        out_specs=pl.BlockSpec((tm, D), lambda i: (i, 0)),
        out_shape=jax.ShapeDtypeStruct((T, D), _F32),
        compiler_params=_params(1),
        name="proj_residual",
    )(y, w, r, g_post)


def _gelu_tanh(x):
    return 0.5 * x * (1.0 + jnp.tanh((2.0 / jnp.pi) ** 0.5 * (x + 0.044715 * (x * x * x))))


def _ffn_kernel(r_ref, halo_ref, gpre_ref, gpost_ref, wg_ref, wv_ref, cwg_ref, cwv_ref,
                cbg_ref, cbv_ref, wd_ref, out_ref, hn_ref, hg_ref, hv_ref, *, tm, rb, tiles_per_seq):
    i = pl.program_id(0)
    c = pl.program_id(1)

    @pl.when(c == 0)
    def _():
        hn_ref[pl.ds(HALO, tm), :] = _rms(r_ref[...], gpre_ref[...]).astype(_BF)
        halo = _rms(halo_ref[...], gpre_ref[...])
        halo = jnp.where(i % tiles_per_seq == 0, 0.0, halo)
        hn_ref[pl.ds(0, HALO), :] = halo.astype(_BF)
        out_ref[...] = jnp.zeros(out_ref.shape, _F32)

    n_blocks = tm // rb
    for q in range(n_blocks):
        for h_ref, w_ref in ((hg_ref, wg_ref), (hv_ref, wv_ref)):
            if q == 0:
                h_ref[0] = _dot(hn_ref[pl.ds(0, HALO + rb), :], w_ref[...])
            else:
                h_ref[q, pl.ds(HALO, rb), :] = _dot(hn_ref[pl.ds(HALO + q * rb, rb), :], w_ref[...])
                h_ref[q, pl.ds(HALO - 8, 8), :] = h_ref[q - 1, pl.ds(HALO + rb - 8, 8), :]

    def conv(h_ref, q, cw_ref, cb_ref):
        acc = cb_ref[...] + h_ref[q, pl.ds(HALO, rb), :] * cw_ref[CONV_W - 1:CONV_W, :]
        for j in range(1, CONV_W):
            acc = acc + h_ref[q, pl.ds(HALO - j, rb), :] * cw_ref[CONV_W - 1 - j:CONV_W - j, :]
        return acc

    for q in range(n_blocks):
        gate = conv(hg_ref, q, cwg_ref, cbg_ref)
        val = conv(hv_ref, q, cwv_ref, cbv_ref)
        act = (_gelu_tanh(gate) * val).astype(_BF)
        out_ref[pl.ds(q * rb, rb), :] += _dot(act, wd_ref[...])

    @pl.when(c == pl.num_programs(1) - 1)
    def _():
        out_ref[...] = r_ref[...] + _rms(out_ref[...], gpost_ref[...])


def _conv_ffn(r, g_pre, g_post, w_up, conv_w, conv_b, w_down, *, layer, seq):
    T, D = r.shape
    F = w_up.shape[1] // 2
    tm = min(1024, seq)
    rb = min(256, tm)
    fc = min(512, F)
    n_f = F // fc
    kern = functools.partial(_ffn_kernel, tm=tm, rb=rb, tiles_per_seq=seq // tm)
    return pl.pallas_call(
        kern,
        grid=(T // tm, n_f),
        in_specs=[
            _resident((tm, D), lambda i, c: (i, 0)),
            pl.BlockSpec((HALO, D), lambda i, c: (jnp.maximum(i * (tm // HALO) - 1, 0), 0)),
            _resident((1, D), lambda i, c: (0, 0)),
            _resident((1, D), lambda i, c: (0, 0)),
            pl.BlockSpec((D, fc), lambda i, c: (layer, c)),
            pl.BlockSpec((D, fc), lambda i, c: (layer, c + n_f)),
            pl.BlockSpec((CONV_W, fc), lambda i, c: (0, c)),
            pl.BlockSpec((CONV_W, fc), lambda i, c: (0, c + n_f)),
            pl.BlockSpec((1, fc), lambda i, c: (0, c)),
            pl.BlockSpec((1, fc), lambda i, c: (0, c + n_f)),
            pl.BlockSpec((fc, D), lambda i, c: (layer * n_f + c, 0)),
        ],
        out_specs=pl.BlockSpec((tm, D), lambda i, c: (i, 0)),
        out_shape=jax.ShapeDtypeStruct((T, D), _F32),
        scratch_shapes=[
            pltpu.VMEM((tm + HALO, D), _BF),
            pltpu.VMEM((tm // rb, rb + HALO, fc), _F32),
            pltpu.VMEM((tm // rb, rb + HALO, fc), _F32),
        ],
        compiler_params=_params(2),
        name="conv_ffn",
    )(r, r, g_pre, g_post, w_up, w_up, conv_w, conv_w, conv_b, conv_b, w_down)


def _ple_kernel(r_ref, p_ref, win_ref, wgate_ref, ge_ref, gr_ref, out_ref):
    r = r_ref[...]
    e = _rms(_dot(p_ref[...].astype(_BF), win_ref[...]), ge_ref[...])
    gate = _sigmoid(_dot(_rms(r, gr_ref[...]).astype(_BF), wgate_ref[...]))
    out_ref[...] = r + gate * e


def _ple_gate(r, p, w_in, w_gate, g_e, g_r):
    T, D = r.shape
    P = p.shape[1]
    tm = 512
    return pl.pallas_call(
        _ple_kernel,
        grid=(T // tm,),
        in_specs=[
            pl.BlockSpec((tm, D), lambda i: (i, 0)),
            pl.BlockSpec((tm, P), lambda i: (i, 0)),
            _resident((P, D), lambda i: (0, 0)),
            _resident((D, D), lambda i: (0, 0)),
            _resident((1, D), lambda i: (0, 0)),
            _resident((1, D), lambda i: (0, 0)),
        ],
        out_specs=pl.BlockSpec((tm, D), lambda i: (i, 0)),
        out_shape=jax.ShapeDtypeStruct((T, D), _F32),
        compiler_params=_params(1),
        name="ple_gate",
    )(r, p, w_in, w_gate, g_e, g_r)


def kernel(x, p, norm_g, hgrn_w_in, hgrn_lb_logits, hgrn_norm_g, hgrn_w_out, gmlp_w_in, gmlp_ln_g,
           gmlp_ln_b, gmlp_w_s, gmlp_b_s, gmlp_w_out, ffn_w_up, ffn_conv_w, ffn_conv_b, ffn_w_down,
           ple_w_in, ple_w_gate, ple_norm_g):
    B, S, D = x.shape
    depth = p.shape[0]
    H = D // LANES
    T = B * S
    row = lambda a: a.reshape(1, -1)

    lbl = hgrn_lb_logits.astype(_F32).reshape(-1, H, LANES).transpose(1, 0, 2)

    F = ffn_w_down.shape[1]
    ffn_up = ffn_w_up.reshape(depth * D, 2 * F)
    ffn_down = ffn_w_down.reshape(depth * F, D)
    ffn_up_bf = ffn_down_bf = None

    r = x.reshape(T, D)
    for i in range(depth):
        j = i // 2
        if i % 2 == 0:
            to_cast = (ffn_up, ffn_down) if ffn_up_bf is None else ()
            y, cast = _hgrn_mixer(r, row(norm_g[i, 0]), _hgrn_weights(hgrn_w_in[j]), lbl, row(hgrn_norm_g[j]), to_cast,
                                  layer=i, batch=B, seq=S)
            if to_cast:
                ffn_up_bf, ffn_down_bf = cast
            w_out = hgrn_w_out[j]
        else:
            y = _gmlp_mixer(r, row(norm_g[i, 0]), gmlp_w_in[j].astype(_BF), row(gmlp_ln_g[j]),
                            row(gmlp_ln_b[j]), gmlp_w_s[j], gmlp_b_s[j].T)
            w_out = gmlp_w_out[j]
        r = _proj_residual(y, w_out.astype(_BF), r, row(norm_g[i, 1]))
        r = _conv_ffn(r, row(norm_g[i, 2]), row(norm_g[i, 3]), ffn_up_bf, ffn_conv_w[i],
                      row(ffn_conv_b[i]), ffn_down_bf, layer=i, seq=S)
        r = _ple_gate(r, p[i].reshape(T, -1), ple_w_in[i].astype(_BF), ple_w_gate[i].astype(_BF),
                      row(ple_norm_g[i, 0]), row(ple_norm_g[i, 1]))
    return r.reshape(B, S, D)
```

```python
import functools

import jax
import jax.numpy as jnp
from jax import lax
from jax.experimental import pallas as pl
from jax.experimental.pallas import tpu as pltpu

EPS = 1e-6
LANES = 128
HGRN_CHUNK = 64
HGRN_SAFE_DECAY = 60.0
HGRN_HEADS_PER_STEP = 2
GMLP_CHUNK = 128
CONV_W = 3
HALO = 16
NORM_ROWS = 128
VMEM_LIMIT = 60 * 1024 * 1024

_BF = jnp.bfloat16
_F32 = jnp.float32


def _params(n_axes):
    return pltpu.CompilerParams(
        dimension_semantics=("arbitrary",) * n_axes,
        vmem_limit_bytes=VMEM_LIMIT,
    )


def _resident(shape, index_map):
    return pl.BlockSpec(shape, index_map, pipeline_mode=pl.Buffered(1))


def _rms(x, g):
    return x * lax.rsqrt(jnp.mean(x * x, axis=-1, keepdims=True) + EPS) * g


def _sigmoid(x):
    return 1.0 / (1.0 + jnp.exp(-x))


def _dot(a, b):
    return jnp.dot(a, b, preferred_element_type=_F32)


def _dot_nt(a, b):
    return lax.dot_general(a, b, (((1,), (1,)), ((), ())), preferred_element_type=_F32)


def _dot_tn(a, b):
    return lax.dot_general(a, b, (((0,), (0,)), ((), ())), preferred_element_type=_F32)


def _chunk_cumsum(x, pos):
    shift = 1
    while shift < HGRN_CHUNK:
        x = x + jnp.where(pos >= shift, pltpu.roll(x, shift, axis=0), 0.0)
        shift *= 2
    return x


def _hgrn_gates(proj, lb):
    q = proj[:, 0:LANES]
    f = lb + (1.0 - lb) * _sigmoid(proj[:, LANES:2 * LANES])
    v = proj[:, 2 * LANES:3 * LANES]
    og = proj[:, 3 * LANES:4 * LANES]
    pos = lax.broadcasted_iota(jnp.int32, f.shape, 0) % HGRN_CHUNK
    return q, 1.0 - f, v, og, _chunk_cumsum(jnp.log(f), pos)


def _hgrn_out(o, og, ng):
    return (_rms(o, ng) * (og * _sigmoid(og))).astype(_BF)


def _hgrn_wcast_kernel(*refs):
    out_ref = refs[-1]
    for s, w_ref in enumerate(refs[:-1]):
        out_ref[:, s * LANES:(s + 1) * LANES] = w_ref[...].astype(_BF)


def _hgrn_weights(w_in):
    D = w_in.shape[0]
    H = D // LANES
    hp = HGRN_HEADS_PER_STEP
    specs = [pl.BlockSpec((D, LANES), lambda p, hh=hh, g=g: (0, g * H + p * hp + hh))
             for hh in range(hp) for g in range(4)]
    return pl.pallas_call(
        _hgrn_wcast_kernel,
        grid=(H // hp,),
        in_specs=specs,
        out_specs=pl.BlockSpec((None, D, hp * 4 * LANES), lambda p: (p, 0, 0)),
        out_shape=jax.ShapeDtypeStruct((H // hp, D, hp * 4 * LANES), _BF),
        compiler_params=_params(1),
        name="hgrn_weights",
    )(*([w_in] * (4 * hp)))


def _hgrn_kernel(r_ref, gpre_ref, win_ref, lbl_ref, ng_ref, *rest, layer, tm, rb, n_cast):
    cast_in, rest = rest[:n_cast], rest[n_cast:]
    y_ref, cast_out = rest[0], rest[1:1 + n_cast]
    hn_ref, st_ref, proj_ref, o_ref, fq_ref, fb_ref = rest[1 + n_cast:]
    n = pl.program_id(1)
    p = pl.program_id(2)
    heads = range(HGRN_HEADS_PER_STEP)
    head_cols = lambda hh, w: slice(hh * w, (hh + 1) * w)

    for src_ref, dst_ref in zip(cast_in, cast_out):
        dst_ref[...] = src_ref[...].astype(_BF)

    @pl.when(p == 0)
    def _():
        def norm_rows(j, carry):
            rows = pl.ds(pl.multiple_of(j * NORM_ROWS, NORM_ROWS), NORM_ROWS)
            hn_ref[rows, :] = _rms(r_ref[rows, :], gpre_ref[...]).astype(_BF)
            return carry

        lax.fori_loop(0, tm // NORM_ROWS, norm_rows, 0)

    @pl.when(n == 0)
    def _():
        for hh in heads:
            st_ref[p * HGRN_HEADS_PER_STEP + hh] = jnp.zeros((LANES, LANES), _F32)

    lbs = []
    for hh in heads:
        lg = lbl_ref[hh]
        ex = jnp.exp(lg - jnp.max(lg, axis=0, keepdims=True))
        lbs.append(jnp.sum(ex[0:layer + 1], axis=0, keepdims=True) / jnp.sum(ex, axis=0, keepdims=True))

    n_blocks = tm // rb
    for qb in range(n_blocks):
        rows = pl.ds(qb * rb, rb)
        proj_ref[rows, :] = _dot(hn_ref[rows, :], win_ref[...])

    row = lax.broadcasted_iota(jnp.int32, (HGRN_CHUNK, HGRN_CHUNK), 0)
    col = lax.broadcasted_iota(jnp.int32, (HGRN_CHUNK, HGRN_CHUNK), 1)
    causal = row >= col

    sts = [st_ref[p * HGRN_HEADS_PER_STEP + hh] for hh in heads]
    any_unsafe = jnp.zeros((1, LANES), _F32)
    for qb in range(n_blocks):
        rows = pl.ds(qb * rb, rb)
        gates = [_hgrn_gates(proj_ref[rows, head_cols(hh, 4 * LANES)], lbs[hh]) for hh in heads]
        pairs = [(c, hh) for c in range(rb // HGRN_CHUNK) for hh in heads]
        q_mid, k_mid, q_dec, k_dec, vb, d_last = {}, {}, {}, {}, {}, {}
        for c, hh in pairs:
            sl = slice(c * HGRN_CHUNK, (c + 1) * HGRN_CHUNK)
            q, k, v, _, bcum = gates[hh]
            b = bcum[sl]
            qc, kc = q[sl], k[sl]
            b_mid = b[HGRN_CHUNK // 2 - 1:HGRN_CHUNK // 2]
            b_last = b[HGRN_CHUNK - 1:HGRN_CHUNK]
            unsafe = b_last < -HGRN_SAFE_DECAY
            any_unsafe = jnp.maximum(any_unsafe, unsafe.astype(_F32))
            q_mid[c, hh] = jnp.where(unsafe, 0.0, qc * jnp.exp(b - b_mid)).astype(_BF)
            k_mid[c, hh] = jnp.where(unsafe, 0.0, kc * jnp.exp(b_mid - b)).astype(_BF)
            q_dec[c, hh] = (qc * jnp.exp(b)).astype(_BF)
            k_dec[c, hh] = (kc * jnp.exp(b_last - b)).astype(_BF)
            vb[c, hh] = v[sl].astype(_BF)
            d_last[c, hh] = jnp.exp(b_last)
        scores = {ph: _dot_nt(q_mid[ph], k_mid[ph]) for ph in pairs}
        st_add = {ph: _dot_tn(vb[ph], k_dec[ph]) for ph in pairs}
        o_intra = {ph: _dot(jnp.where(causal, scores[ph], 0.0).astype(_BF), vb[ph]) for ph in pairs}
        st_in = {}
        for c, hh in pairs:
            st_in[c, hh] = sts[hh]
            sts[hh] = sts[hh] * d_last[c, hh] + st_add[c, hh]
        o_inter = {ph: _dot_nt(q_dec[ph], st_in[ph].astype(_BF)) for ph in pairs}
        for hh in heads:
            o = jnp.concatenate([o_intra[c, hh] + o_inter[c, hh] for c in range(rb // HGRN_CHUNK)], axis=0)
            o_ref[hh, rows, :] = o
            y_ref[rows, head_cols(hh, LANES)] = _hgrn_out(o, gates[hh][3], ng_ref[...])
    for hh in heads:
        st_ref[p * HGRN_HEADS_PER_STEP + hh] = sts[hh]

    @pl.when(jnp.max(any_unsafe) > 0.0)
    def _():
        s_idx = lax.broadcasted_iota(jnp.int32, (HGRN_CHUNK, 1), 0)

        for hh in heads:
            def chunk_body(c, carry, hh=hh):
                base = pl.multiple_of(c * HGRN_CHUNK, HGRN_CHUNK)
                chunk = pl.ds(base, HGRN_CHUNK)
                q, k, v, og, b = _hgrn_gates(proj_ref[chunk, head_cols(hh, 4 * LANES)], lbs[hh])
                unsafe = b[HGRN_CHUNK - 1:HGRN_CHUNK] < -HGRN_SAFE_DECAY
                fq_ref[...] = jnp.where(unsafe, q, 0.0)
                fb_ref[...] = b

                def row_body(t, carry2):
                    qt = fq_ref[pl.ds(t, 1), :]
                    bt = fb_ref[pl.ds(t, 1), :]
                    w = qt * k * jnp.exp(jnp.minimum(bt - b, 0.0))
                    w = jnp.where(s_idx <= t, w, 0.0)
                    sc = jnp.sum(w, axis=1, keepdims=True)
                    o_ref[hh, pl.ds(base + t, 1), :] += jnp.sum(sc * v, axis=0, keepdims=True)
                    return carry2

                lax.fori_loop(0, HGRN_CHUNK, row_body, 0)
                y_ref[chunk, head_cols(hh, LANES)] = _hgrn_out(o_ref[hh, chunk, :], og, ng_ref[...])
                return carry

            lax.fori_loop(0, tm // HGRN_CHUNK, chunk_body, 0)


def _hgrn_mixer(r, g_pre, w_in_groups, lb_logits_heads, norm_g, to_cast, *, layer, batch, seq):
    T, D = r.shape
    H = D // LANES
    hp = HGRN_HEADS_PER_STEP
    tm = min(1024, seq)
    rb = min(256, tm)
    n_s = seq // tm
    n_steps = batch * n_s * (H // hp)
    step = lambda b, n, p: (b * n_s + n) * (H // hp) + p
    cast_specs = []
    for a in to_cast:
        rows = a.shape[0] // n_steps
        assert rows * n_steps == a.shape[0] and rows % 16 == 0, a.shape
        cast_specs.append(pl.BlockSpec((rows, a.shape[1]), lambda b, n, p: (step(b, n, p), 0)))
    kern = functools.partial(_hgrn_kernel, layer=layer, tm=tm, rb=rb, n_cast=len(to_cast))
    out = pl.pallas_call(
        kern,
        grid=(batch, n_s, H // hp),
        in_specs=[
            pl.BlockSpec((tm, D), lambda b, n, p: (b * n_s + n, 0)),
            _resident((1, D), lambda b, n, p: (0, 0)),
            pl.BlockSpec((None, D, hp * 4 * LANES), lambda b, n, p: (p, 0, 0)),
            pl.BlockSpec((hp,) + lb_logits_heads.shape[1:], lambda b, n, p: (p, 0, 0)),
            _resident((1, LANES), lambda b, n, p: (0, 0)),
        ] + cast_specs,
        out_specs=[pl.BlockSpec((tm, hp * LANES), lambda b, n, p: (b * n_s + n, p))] + cast_specs,
        out_shape=[jax.ShapeDtypeStruct((T, D), _BF)]
        + [jax.ShapeDtypeStruct(a.shape, _BF) for a in to_cast],
        scratch_shapes=[
            pltpu.VMEM((tm, D), _BF),
            pltpu.VMEM((H, LANES, LANES), _F32),
            pltpu.VMEM((tm, hp * 4 * LANES), _F32),
            pltpu.VMEM((hp, tm, LANES), _F32),
            pltpu.VMEM((HGRN_CHUNK, LANES), _F32),
            pltpu.VMEM((HGRN_CHUNK, LANES), _F32),
        ],
        compiler_params=_params(3),
        name="hgrn_mixer",
    )(r, g_pre, w_in_groups, lb_logits_heads, norm_g, *to_cast)
    return out[0], out[1:]


def _gelu_erf(x):
    return 0.5 * x * (1.0 + lax.erf(x * (2.0 ** -0.5)))


def _gmlp_kernel(r_ref, gpre_ref, win_ref, lng_ref, lnb_ref, ws_ref, bst_ref, y_ref, *, tm):
    D = r_ref.shape[1]
    hn = _rms(r_ref[...], gpre_ref[...]).astype(_BF)
    u = _gelu_erf(_dot(hn, win_ref[:, 0:D]))
    v = _gelu_erf(_dot(hn, win_ref[:, D:2 * D]))
    mu = jnp.mean(v, axis=-1, keepdims=True)
    vc = v - mu
    var = jnp.mean(vc * vc, axis=-1, keepdims=True)
    vb = (vc * lax.rsqrt(var + EPS) * lng_ref[...] + lnb_ref[...]).astype(_BF)

    row = lax.broadcasted_iota(jnp.int32, (GMLP_CHUNK, GMLP_CHUNK), 0)
    col = lax.broadcasted_iota(jnp.int32, (GMLP_CHUNK, GMLP_CHUNK), 1)
    causal = row >= col
    for g in range(D // LANES):
        cs = slice(g * LANES, (g + 1) * LANES)
        w = jnp.where(causal, ws_ref[g], 0.0).astype(_BF)
        bias = bst_ref[:, g:g + 1]
        for c in range(tm // GMLP_CHUNK):
            rs = slice(c * GMLP_CHUNK, (c + 1) * GMLP_CHUNK)
            mixed = _dot(w, vb[rs, cs]) + bias
            y_ref[rs, cs] = (u[rs, cs] * mixed).astype(_BF)


def _gmlp_mixer(r, g_pre, w_in, ln_g, ln_b, w_s, b_s_t):
    T, D = r.shape
    G = D // LANES
    tm = 256
    return pl.pallas_call(
        functools.partial(_gmlp_kernel, tm=tm),
        grid=(T // tm,),
        in_specs=[
            pl.BlockSpec((tm, D), lambda i: (i, 0)),
            _resident((1, D), lambda i: (0, 0)),
            _resident((D, 2 * D), lambda i: (0, 0)),
            _resident((1, D), lambda i: (0, 0)),
            _resident((1, D), lambda i: (0, 0)),
            _resident((G, GMLP_CHUNK, GMLP_CHUNK), lambda i: (0, 0, 0)),
            _resident((GMLP_CHUNK, G), lambda i: (0, 0)),
        ],
        out_specs=pl.BlockSpec((tm, D), lambda i: (i, 0)),
        out_shape=jax.ShapeDtypeStruct((T, D), _BF),
        compiler_params=_params(1),
        name="gmlp_mixer",
    )(r, g_pre, w_in, ln_g, ln_b, w_s, b_s_t)


def _proj_kernel(y_ref, w_ref, r_ref, g_ref, out_ref):
    out_ref[...] = r_ref[...] + _rms(_dot(y_ref[...], w_ref[...]), g_ref[...])


def _proj_residual(y, w, r, g_post):
    T, D = r.shape
    tm = 512
    return pl.pallas_call(
        _proj_kernel,
        grid=(T // tm,),
        in_specs=[
            pl.BlockSpec((tm, D), lambda i: (i, 0)),
            _resident((D, D), lambda i: (0, 0)),
            pl.BlockSpec((tm, D), lambda i: (i, 0)),
            _resident((1, D), lambda i: (0, 0)),
        ],
        out_specs=pl.BlockSpec((tm, D), lambda i: (i, 0)),
        out_shape=jax.ShapeDtypeStruct((T, D), _F32),
        compiler_params=_params(1),
        name="proj_residual",
    )(y, w, r, g_post)


def _gelu_tanh(x):
    return 0.5 * x * (1.0 + jnp.tanh((2.0 / jnp.pi) ** 0.5 * (x + 0.044715 * (x * x * x))))


def _ffn_kernel(r_ref, halo_ref, gpre_ref, gpost_ref, wg_ref, wv_ref, cwg_ref, cwv_ref,
                cbg_ref, cbv_ref, wd_ref, out_ref, hn_ref, hg_ref, hv_ref, *, tm, rb, n_chunks, tiles_per_seq):
    i = pl.program_id(0)
    c = pl.program_id(1)
    n_blocks = tm // rb

    def conv(h_ref, q, cw_ref, cb_ref):
        acc = cb_ref[...] + h_ref[q, pl.ds(HALO, rb), :] * cw_ref[CONV_W - 1:CONV_W, :]
        for j in range(1, CONV_W):
            acc = acc + h_ref[q, pl.ds(HALO - j, rb), :] * cw_ref[CONV_W - 1 - j:CONV_W - j, :]
        return acc

    def chunk_step(first, last):
        if first:
            halo = _rms(halo_ref[...], gpre_ref[...])
            halo = jnp.where(i % tiles_per_seq == 0, 0.0, halo)
            hn_ref[pl.ds(0, HALO), :] = halo.astype(_BF)
        for q in range(n_blocks):
            if first:
                hn_ref[pl.ds(HALO + q * rb, rb), :] = _rms(r_ref[pl.ds(q * rb, rb), :], gpre_ref[...]).astype(_BF)
            for h_ref, w_ref in ((hg_ref, wg_ref), (hv_ref, wv_ref)):
                if q == 0:
                    h_ref[0] = _dot(hn_ref[pl.ds(0, HALO + rb), :], w_ref[...])
                else:
                    h_ref[q, pl.ds(HALO, rb), :] = _dot(hn_ref[pl.ds(HALO + q * rb, rb), :], w_ref[...])
                    h_ref[q, pl.ds(HALO - 8, 8), :] = h_ref[q - 1, pl.ds(HALO + rb - 8, 8), :]

        for q in range(n_blocks):
            rows = pl.ds(q * rb, rb)
            gate = conv(hg_ref, q, cwg_ref, cbg_ref)
            val = conv(hv_ref, q, cwv_ref, cbv_ref)
            act = (_gelu_tanh(gate) * val).astype(_BF)
            down = _dot(act, wd_ref[...])
            if not first:
                down = out_ref[rows, :] + down
            if last:
                down = r_ref[rows, :] + _rms(down, gpost_ref[...])
            out_ref[rows, :] = down

    n_f = pl.num_programs(1)
    pl.when(c == 0)(lambda: chunk_step(True, n_chunks == 1))
    if n_chunks > 1:
        pl.when(c == n_f - 1)(lambda: chunk_step(False, True))
    if n_chunks > 2:
        pl.when(jnp.logical_and(c > 0, c < n_f - 1))(lambda: chunk_step(False, False))


def _conv_ffn(r, g_pre, g_post, w_up, conv_w, conv_b, w_down, *, layer, seq):
    T, D = r.shape
    F = w_up.shape[1] // 2
    tm = min(1024, seq)
    rb = min(256, tm)
    fc = min(512, F)
    n_f = F // fc
    kern = functools.partial(_ffn_kernel, tm=tm, rb=rb, n_chunks=n_f, tiles_per_seq=seq // tm)
    return pl.pallas_call(
        kern,
        grid=(T // tm, n_f),
        in_specs=[
            _resident((tm, D), lambda i, c: (i, 0)),
            pl.BlockSpec((HALO, D), lambda i, c: (jnp.maximum(i * (tm // HALO) - 1, 0), 0)),
            _resident((1, D), lambda i, c: (0, 0)),
            _resident((1, D), lambda i, c: (0, 0)),
            pl.BlockSpec((D, fc), lambda i, c: (layer, c)),
            pl.BlockSpec((D, fc), lambda i, c: (layer, c + n_f)),
            pl.BlockSpec((CONV_W, fc), lambda i, c: (0, c)),
            pl.BlockSpec((CONV_W, fc), lambda i, c: (0, c + n_f)),
            pl.BlockSpec((1, fc), lambda i, c: (0, c)),
            pl.BlockSpec((1, fc), lambda i, c: (0, c + n_f)),
            pl.BlockSpec((fc, D), lambda i, c: (layer * n_f + c, 0)),
        ],
        out_specs=pl.BlockSpec((tm, D), lambda i, c: (i, 0)),
        out_shape=jax.ShapeDtypeStruct((T, D), _F32),
        scratch_shapes=[
            pltpu.VMEM((tm + HALO, D), _BF),
            pltpu.VMEM((tm // rb, rb + HALO, fc), _F32),
            pltpu.VMEM((tm // rb, rb + HALO, fc), _F32),
        ],
        compiler_params=_params(2),
        name="conv_ffn",
    )(r, r, g_pre, g_post, w_up, w_up, conv_w, conv_w, conv_b, conv_b, w_down)


def _ple_kernel(r_ref, p_ref, win_ref, wgate_ref, ge_ref, gr_ref, out_ref):
    r = r_ref[...]
    e = _rms(_dot(p_ref[...].astype(_BF), win_ref[...]), ge_ref[...])
    gate = _sigmoid(_dot(_rms(r, gr_ref[...]).astype(_BF), wgate_ref[...]))
    out_ref[...] = r + gate * e


def _ple_gate(r, p, w_in, w_gate, g_e, g_r):
    T, D = r.shape
    P = p.shape[1]
    tm = 512
    return pl.pallas_call(
        _ple_kernel,
        grid=(T // tm,),
        in_specs=[
            pl.BlockSpec((tm, D), lambda i: (i, 0)),
            pl.BlockSpec((tm, P), lambda i: (i, 0)),
            _resident((P, D), lambda i: (0, 0)),
            _resident((D, D), lambda i: (0, 0)),
            _resident((1, D), lambda i: (0, 0)),
            _resident((1, D), lambda i: (0, 0)),
        ],
        out_specs=pl.BlockSpec((tm, D), lambda i: (i, 0)),
        out_shape=jax.ShapeDtypeStruct((T, D), _F32),
        compiler_params=_params(1),
        name="ple_gate",
    )(r, p, w_in, w_gate, g_e, g_r)


def kernel(x, p, norm_g, hgrn_w_in, hgrn_lb_logits, hgrn_norm_g, hgrn_w_out, gmlp_w_in, gmlp_ln_g,
           gmlp_ln_b, gmlp_w_s, gmlp_b_s, gmlp_w_out, ffn_w_up, ffn_conv_w, ffn_conv_b, ffn_w_down,
           ple_w_in, ple_w_gate, ple_norm_g):
    B, S, D = x.shape
    depth = p.shape[0]
    H = D // LANES
    T = B * S
    row = lambda a: a.reshape(1, -1)

    lbl = hgrn_lb_logits.astype(_F32).reshape(-1, H, LANES).transpose(1, 0, 2)

    F = ffn_w_down.shape[1]
    ffn_up = ffn_w_up.reshape(depth * D, 2 * F)
    ffn_down = ffn_w_down.reshape(depth * F, D)
    ffn_up_bf = ffn_down_bf = None

    r = x.reshape(T, D)
    for i in range(depth):
        j = i // 2
        if i % 2 == 0:
            to_cast = (ffn_up, ffn_down) if ffn_up_bf is None else ()
            y, cast = _hgrn_mixer(r, row(norm_g[i, 0]), _hgrn_weights(hgrn_w_in[j]), lbl, row(hgrn_norm_g[j]), to_cast,
                                  layer=i, batch=B, seq=S)
            if to_cast:
                ffn_up_bf, ffn_down_bf = cast
            w_out = hgrn_w_out[j]
        else:
            y = _gmlp_mixer(r, row(norm_g[i, 0]), gmlp_w_in[j].astype(_BF), row(gmlp_ln_g[j]),
                            row(gmlp_ln_b[j]), gmlp_w_s[j], gmlp_b_s[j].T)
            w_out = gmlp_w_out[j]
        r = _proj_residual(y, w_out.astype(_BF), r, row(norm_g[i, 1]))
        r = _conv_ffn(r, row(norm_g[i, 2]), row(norm_g[i, 3]), ffn_up_bf, ffn_conv_w[i],
                      row(ffn_conv_b[i]), ffn_down_bf, layer=i, seq=S)
        r = _ple_gate(r, p[i].reshape(T, -1), ple_w_in[i].astype(_BF), ple_w_gate[i].astype(_BF),
                      row(ple_norm_g[i, 0]), row(ple_norm_g[i, 1]))
    return r.reshape(B, S, D)
```

```python
import functools

import jax
import jax.numpy as jnp
from jax import lax
from jax.experimental import pallas as pl
from jax.experimental.pallas import tpu as pltpu

EPS = 1e-6
LANES = 128
HGRN_CHUNK = 64
HGRN_SAFE_DECAY = 60.0
HGRN_HEADS_PER_STEP = 2
GMLP_CHUNK = 128
CONV_W = 3
HALO = 16
NORM_ROWS = 128
VMEM_LIMIT = 60 * 1024 * 1024

_BF = jnp.bfloat16
_F32 = jnp.float32


def _params(n_axes):
    return pltpu.CompilerParams(
        dimension_semantics=("arbitrary",) * n_axes,
        vmem_limit_bytes=VMEM_LIMIT,
    )


def _resident(shape, index_map):
    return pl.BlockSpec(shape, index_map, pipeline_mode=pl.Buffered(1))


def _rms(x, g):
    return x * lax.rsqrt(jnp.mean(x * x, axis=-1, keepdims=True) + EPS) * g


def _sigmoid(x):
    return 1.0 / (1.0 + jnp.exp(-x))


def _dot(a, b):
    return jnp.dot(a, b, preferred_element_type=_F32)


def _dot_nt(a, b):
    return lax.dot_general(a, b, (((1,), (1,)), ((), ())), preferred_element_type=_F32)


def _dot_tn(a, b):
    return lax.dot_general(a, b, (((0,), (0,)), ((), ())), preferred_element_type=_F32)


def _chunk_cumsum(x, pos):
    shift = 1
    while shift < HGRN_CHUNK:
        x = x + jnp.where(pos >= shift, pltpu.roll(x, shift, axis=0), 0.0)
        shift *= 2
    return x


def _hgrn_gates(proj, lb):
    q = proj[:, 0:LANES]
    f = lb + (1.0 - lb) * _sigmoid(proj[:, LANES:2 * LANES])
    v = proj[:, 2 * LANES:3 * LANES]
    og = proj[:, 3 * LANES:4 * LANES]
    pos = lax.broadcasted_iota(jnp.int32, f.shape, 0) % HGRN_CHUNK
    return q, 1.0 - f, v, og, _chunk_cumsum(jnp.log(f), pos)


def _hgrn_out(o, og, ng):
    return (_rms(o, ng) * (og * _sigmoid(og))).astype(_BF)


def _hgrn_wcast_kernel(*refs):
    out_ref = refs[-1]
    for s, w_ref in enumerate(refs[:-1]):
        out_ref[:, s * LANES:(s + 1) * LANES] = w_ref[...].astype(_BF)


def _hgrn_weights(w_in):
    D = w_in.shape[0]
    H = D // LANES
    hp = HGRN_HEADS_PER_STEP
    specs = [pl.BlockSpec((D, LANES), lambda p, hh=hh, g=g: (0, g * H + p * hp + hh))
             for hh in range(hp) for g in range(4)]
    return pl.pallas_call(
        _hgrn_wcast_kernel,
        grid=(H // hp,),
        in_specs=specs,
        out_specs=pl.BlockSpec((None, D, hp * 4 * LANES), lambda p: (p, 0, 0)),
        out_shape=jax.ShapeDtypeStruct((H // hp, D, hp * 4 * LANES), _BF),
        compiler_params=_params(1),
        name="hgrn_weights",
    )(*([w_in] * (4 * hp)))


def _hgrn_kernel(r_ref, gpre_ref, win_ref, lbl_ref, ng_ref, *rest, layer, tm, rb, n_cast):
    cast_in, rest = rest[:n_cast], rest[n_cast:]
    y_ref, cast_out = rest[0], rest[1:1 + n_cast]
    hn_ref, st_ref, proj_ref, o_ref, fq_ref, fb_ref = rest[1 + n_cast:]
    n = pl.program_id(1)
    p = pl.program_id(2)
    heads = range(HGRN_HEADS_PER_STEP)
    head_cols = lambda hh, w: slice(hh * w, (hh + 1) * w)

    for src_ref, dst_ref in zip(cast_in, cast_out):
        dst_ref[...] = src_ref[...].astype(_BF)

    @pl.when(p == 0)
    def _():
        def norm_rows(j, carry):
            rows = pl.ds(pl.multiple_of(j * NORM_ROWS, NORM_ROWS), NORM_ROWS)
            hn_ref[rows, :] = _rms(r_ref[rows, :], gpre_ref[...]).astype(_BF)
            return carry

        lax.fori_loop(0, tm // NORM_ROWS, norm_rows, 0)

    @pl.when(n == 0)
    def _():
        for hh in heads:
            st_ref[p * HGRN_HEADS_PER_STEP + hh] = jnp.zeros((LANES, LANES), _F32)

    lbs = []
    for hh in heads:
        lg = lbl_ref[hh]
        ex = jnp.exp(lg - jnp.max(lg, axis=0, keepdims=True))
        lbs.append(jnp.sum(ex[0:layer + 1], axis=0, keepdims=True) / jnp.sum(ex, axis=0, keepdims=True))

    n_blocks = tm // rb
    for qb in range(n_blocks):
        rows = pl.ds(qb * rb, rb)
        proj_ref[rows, :] = _dot(hn_ref[rows, :], win_ref[...])

    row = lax.broadcasted_iota(jnp.int32, (HGRN_CHUNK, HGRN_CHUNK), 0)
    col = lax.broadcasted_iota(jnp.int32, (HGRN_CHUNK, HGRN_CHUNK), 1)
    causal = row >= col

    sts = [st_ref[p * HGRN_HEADS_PER_STEP + hh] for hh in heads]
    any_unsafe = jnp.zeros((1, LANES), _F32)
    for qb in range(n_blocks):
        rows = pl.ds(qb * rb, rb)
        gates = [_hgrn_gates(proj_ref[rows, head_cols(hh, 4 * LANES)], lbs[hh]) for hh in heads]
        pairs = [(c, hh) for c in range(rb // HGRN_CHUNK) for hh in heads]
        q_mid, k_mid, q_dec, k_dec, vb, d_last = {}, {}, {}, {}, {}, {}
        for c, hh in pairs:
            sl = slice(c * HGRN_CHUNK, (c + 1) * HGRN_CHUNK)
            q, k, v, _, bcum = gates[hh]
            b = bcum[sl]
            qc, kc = q[sl], k[sl]
            b_mid = b[HGRN_CHUNK // 2 - 1:HGRN_CHUNK // 2]
            b_last = b[HGRN_CHUNK - 1:HGRN_CHUNK]
            unsafe = b_last < -HGRN_SAFE_DECAY
            any_unsafe = jnp.maximum(any_unsafe, unsafe.astype(_F32))
            q_mid[c, hh] = jnp.where(unsafe, 0.0, qc * jnp.exp(b - b_mid)).astype(_BF)
            k_mid[c, hh] = jnp.where(unsafe, 0.0, kc * jnp.exp(b_mid - b)).astype(_BF)
            q_dec[c, hh] = (qc * jnp.exp(b)).astype(_BF)
            k_dec[c, hh] = (kc * jnp.exp(b_last - b)).astype(_BF)
            vb[c, hh] = v[sl].astype(_BF)
            d_last[c, hh] = jnp.exp(b_last)
        scores = {ph: _dot_nt(q_mid[ph], k_mid[ph]) for ph in pairs}
        st_add = {ph: _dot_tn(vb[ph], k_dec[ph]) for ph in pairs}
        o_intra = {ph: _dot(jnp.where(causal, scores[ph], 0.0).astype(_BF), vb[ph]) for ph in pairs}
        st_in = {}
        for c, hh in pairs:
            st_in[c, hh] = sts[hh]
            sts[hh] = sts[hh] * d_last[c, hh] + st_add[c, hh]
        o_inter = {ph: _dot_nt(q_dec[ph], st_in[ph].astype(_BF)) for ph in pairs}
        for hh in heads:
            o = jnp.concatenate([o_intra[c, hh] + o_inter[c, hh] for c in range(rb // HGRN_CHUNK)], axis=0)
            o_ref[hh, rows, :] = o
            y_ref[rows, head_cols(hh, LANES)] = _hgrn_out(o, gates[hh][3], ng_ref[...])
    for hh in heads:
        st_ref[p * HGRN_HEADS_PER_STEP + hh] = sts[hh]

    @pl.when(jnp.max(any_unsafe) > 0.0)
    def _():
        s_idx = lax.broadcasted_iota(jnp.int32, (HGRN_CHUNK, 1), 0)

        for hh in heads:
            def chunk_body(c, carry, hh=hh):
                base = pl.multiple_of(c * HGRN_CHUNK, HGRN_CHUNK)
                chunk = pl.ds(base, HGRN_CHUNK)
                q, k, v, og, b = _hgrn_gates(proj_ref[chunk, head_cols(hh, 4 * LANES)], lbs[hh])
                unsafe = b[HGRN_CHUNK - 1:HGRN_CHUNK] < -HGRN_SAFE_DECAY
                fq_ref[...] = jnp.where(unsafe, q, 0.0)
                fb_ref[...] = b

                def row_body(t, carry2):
                    qt = fq_ref[pl.ds(t, 1), :]
                    bt = fb_ref[pl.ds(t, 1), :]
                    w = qt * k * jnp.exp(jnp.minimum(bt - b, 0.0))
                    w = jnp.where(s_idx <= t, w, 0.0)
                    sc = jnp.sum(w, axis=1, keepdims=True)
                    o_ref[hh, pl.ds(base + t, 1), :] += jnp.sum(sc * v, axis=0, keepdims=True)
                    return carry2

                lax.fori_loop(0, HGRN_CHUNK, row_body, 0)
                y_ref[chunk, head_cols(hh, LANES)] = _hgrn_out(o_ref[hh, chunk, :], og, ng_ref[...])
                return carry

            lax.fori_loop(0, tm // HGRN_CHUNK, chunk_body, 0)


def _hgrn_mixer(r, g_pre, w_in_groups, lb_logits_heads, norm_g, to_cast, *, layer, batch, seq):
    T, D = r.shape
    H = D // LANES
    hp = HGRN_HEADS_PER_STEP
    tm = min(1024, seq)
    rb = min(256, tm)
    n_s = seq // tm
    n_steps = batch * n_s * (H // hp)
    step = lambda b, n, p: (b * n_s + n) * (H // hp) + p
    cast_specs = []
    for a in to_cast:
        rows = a.shape[0] // n_steps
        assert rows * n_steps == a.shape[0] and rows % 16 == 0, a.shape
        cast_specs.append(pl.BlockSpec((rows, a.shape[1]), lambda b, n, p: (step(b, n, p), 0)))
    kern = functools.partial(_hgrn_kernel, layer=layer, tm=tm, rb=rb, n_cast=len(to_cast))
    out = pl.pallas_call(
        kern,
        grid=(batch, n_s, H // hp),
        in_specs=[
            pl.BlockSpec((tm, D), lambda b, n, p: (b * n_s + n, 0)),
            _resident((1, D), lambda b, n, p: (0, 0)),
            pl.BlockSpec((None, D, hp * 4 * LANES), lambda b, n, p: (p, 0, 0)),
            pl.BlockSpec((hp,) + lb_logits_heads.shape[1:], lambda b, n, p: (p, 0, 0)),
            _resident((1, LANES), lambda b, n, p: (0, 0)),
        ] + cast_specs,
        out_specs=[pl.BlockSpec((tm, hp * LANES), lambda b, n, p: (b * n_s + n, p))] + cast_specs,
        out_shape=[jax.ShapeDtypeStruct((T, D), _BF)]
        + [jax.ShapeDtypeStruct(a.shape, _BF) for a in to_cast],
        scratch_shapes=[
            pltpu.VMEM((tm, D), _BF),
            pltpu.VMEM((H, LANES, LANES), _F32),
            pltpu.VMEM((tm, hp * 4 * LANES), _F32),
            pltpu.VMEM((hp, tm, LANES), _F32),
            pltpu.VMEM((HGRN_CHUNK, LANES), _F32),
            pltpu.VMEM((HGRN_CHUNK, LANES), _F32),
        ],
        compiler_params=_params(3),
        name="hgrn_mixer",
    )(r, g_pre, w_in_groups, lb_logits_heads, norm_g, *to_cast)
    return out[0], out[1:]


def _gelu_erf(x):
    return 0.5 * x * (1.0 + lax.erf(x * (2.0 ** -0.5)))


def _gmlp_kernel(r_ref, gpre_ref, win_ref, lng_ref, lnb_ref, ws_ref, bst_ref, y_ref, *, tm):
    D = r_ref.shape[1]
    hn = _rms(r_ref[...], gpre_ref[...]).astype(_BF)
    u = _gelu_erf(_dot(hn, win_ref[:, 0:D]))
    v = _gelu_erf(_dot(hn, win_ref[:, D:2 * D]))
    mu = jnp.mean(v, axis=-1, keepdims=True)
    vc = v - mu
    var = jnp.mean(vc * vc, axis=-1, keepdims=True)
    vb = (vc * lax.rsqrt(var + EPS) * lng_ref[...] + lnb_ref[...]).astype(_BF)

    row = lax.broadcasted_iota(jnp.int32, (GMLP_CHUNK, GMLP_CHUNK), 0)
    col = lax.broadcasted_iota(jnp.int32, (GMLP_CHUNK, GMLP_CHUNK), 1)
    causal = row >= col
    for g in range(D // LANES):
        cs = slice(g * LANES, (g + 1) * LANES)
        w = jnp.where(causal, ws_ref[g], 0.0).astype(_BF)
        bias = bst_ref[:, g:g + 1]
        for c in range(tm // GMLP_CHUNK):
            rs = slice(c * GMLP_CHUNK, (c + 1) * GMLP_CHUNK)
            mixed = _dot(w, vb[rs, cs]) + bias
            y_ref[rs, cs] = (u[rs, cs] * mixed).astype(_BF)


def _gmlp_mixer(r, g_pre, w_in, ln_g, ln_b, w_s, b_s_t):
    T, D = r.shape
    G = D // LANES
    tm = 256
    return pl.pallas_call(
        functools.partial(_gmlp_kernel, tm=tm),
        grid=(T // tm,),
        in_specs=[
            pl.BlockSpec((tm, D), lambda i: (i, 0)),
            _resident((1, D), lambda i: (0, 0)),
            _resident((D, 2 * D), lambda i: (0, 0)),
            _resident((1, D), lambda i: (0, 0)),
            _resident((1, D), lambda i: (0, 0)),
            _resident((G, GMLP_CHUNK, GMLP_CHUNK), lambda i: (0, 0, 0)),
            _resident((GMLP_CHUNK, G), lambda i: (0, 0)),
        ],
        out_specs=pl.BlockSpec((tm, D), lambda i: (i, 0)),
        out_shape=jax.ShapeDtypeStruct((T, D), _BF),
        compiler_params=_params(1),
        name="gmlp_mixer",
    )(r, g_pre, w_in, ln_g, ln_b, w_s, b_s_t)


def _proj_kernel(y_ref, w_ref, r_ref, g_ref, out_ref):
    out_ref[...] = r_ref[...] + _rms(_dot(y_ref[...], w_ref[...]), g_ref[...])


def _proj_residual(y, w, r, g_post):
    T, D = r.shape
    tm = 512
    return pl.pallas_call(
        _proj_kernel,
        grid=(T // tm,),
        in_specs=[
            pl.BlockSpec((tm, D), lambda i: (i, 0)),
            _resident((D, D), lambda i: (0, 0)),
            pl.BlockSpec((tm, D), lambda i: (i, 0)),
            _resident((1, D), lambda i: (0, 0)),
        ],
        out_specs=pl.BlockSpec((tm, D), lambda i: (i, 0)),
        out_shape=jax.ShapeDtypeStruct((T, D), _F32),
        compiler_params=_params(1),
        name="proj_residual",
    )(y, w, r, g_post)


def _gelu_tanh(x):
    return 0.5 * x * (1.0 + jnp.tanh((2.0 / jnp.pi) ** 0.5 * (x + 0.044715 * (x * x * x))))


def _ffn_kernel(r_ref, halo_ref, gpre_ref, gpost_ref, wg_ref, wv_ref, cwg_ref, cwv_ref,
                cbg_ref, cbv_ref, wd_ref, out_ref, hn_ref, hg_ref, hv_ref, *, tm, rb, n_chunks, tiles_per_seq):
    i = pl.program_id(0)
    c = pl.program_id(1)
    n_blocks = tm // rb

    def conv(h_ref, q, cw_ref, cb_ref):
        acc = cb_ref[...] + h_ref[q, pl.ds(HALO, rb), :] * cw_ref[CONV_W - 1:CONV_W, :]
        for j in range(1, CONV_W):
            acc = acc + h_ref[q, pl.ds(HALO - j, rb), :] * cw_ref[CONV_W - 1 - j:CONV_W - j, :]
        return acc

    def chunk_step(first, last):
        if first:
            halo = _rms(halo_ref[...], gpre_ref[...])
            halo = jnp.where(i % tiles_per_seq == 0, 0.0, halo)
            hn_ref[pl.ds(0, HALO), :] = halo.astype(_BF)
        for q in range(n_blocks):
            if first:
                hn_ref[pl.ds(HALO + q * rb, rb), :] = _rms(r_ref[pl.ds(q * rb, rb), :], gpre_ref[...]).astype(_BF)
            for h_ref, w_ref in ((hg_ref, wg_ref), (hv_ref, wv_ref)):
                if q == 0:
                    h_ref[0] = _dot(hn_ref[pl.ds(0, HALO + rb), :], w_ref[...])
                else:
                    h_ref[q, pl.ds(HALO, rb), :] = _dot(hn_ref[pl.ds(HALO + q * rb, rb), :], w_ref[...])
                    h_ref[q, pl.ds(HALO - 8, 8), :] = h_ref[q - 1, pl.ds(HALO + rb - 8, 8), :]

        for q in range(n_blocks):
            rows = pl.ds(q * rb, rb)
            gate = conv(hg_ref, q, cwg_ref, cbg_ref)
            val = conv(hv_ref, q, cwv_ref, cbv_ref)
            act = (_gelu_tanh(gate) * val).astype(_BF)
            down = _dot(act, wd_ref[...])
            if not first:
                down = out_ref[rows, :] + down
            if last:
                down = r_ref[rows, :] + _rms(down, gpost_ref[...])
            out_ref[rows, :] = down

    n_f = pl.num_programs(1)
    pl.when(c == 0)(lambda: chunk_step(True, n_chunks == 1))
    if n_chunks > 1:
        pl.when(c == n_f - 1)(lambda: chunk_step(False, True))
    if n_chunks > 2:
        pl.when(jnp.logical_and(c > 0, c < n_f - 1))(lambda: chunk_step(False, False))


def _conv_ffn(r, g_pre, g_post, w_up, conv_w, conv_b, w_down, *, layer, seq):
    T, D = r.shape
    F = w_up.shape[1] // 2
    tm = min(512, seq)
    rb = min(256, tm)
    fc = min(512, F)
    n_f = F // fc
    kern = functools.partial(_ffn_kernel, tm=tm, rb=rb, n_chunks=n_f, tiles_per_seq=seq // tm)
    return pl.pallas_call(
        kern,
        grid=(T // tm, n_f),
        in_specs=[
            pl.BlockSpec((tm, D), lambda i, c: (i, 0)),
            pl.BlockSpec((HALO, D), lambda i, c: (jnp.maximum(i * (tm // HALO) - 1, 0), 0)),
            _resident((1, D), lambda i, c: (0, 0)),
            _resident((1, D), lambda i, c: (0, 0)),
            pl.BlockSpec((D, fc), lambda i, c: (layer, c)),
            pl.BlockSpec((D, fc), lambda i, c: (layer, c + n_f)),
            pl.BlockSpec((CONV_W, fc), lambda i, c: (0, c)),
            pl.BlockSpec((CONV_W, fc), lambda i, c: (0, c + n_f)),
            pl.BlockSpec((1, fc), lambda i, c: (0, c)),
            pl.BlockSpec((1, fc), lambda i, c: (0, c + n_f)),
            pl.BlockSpec((fc, D), lambda i, c: (layer * n_f + c, 0)),
        ],
        out_specs=pl.BlockSpec((tm, D), lambda i, c: (i, 0)),
        out_shape=jax.ShapeDtypeStruct((T, D), _F32),
        scratch_shapes=[
            pltpu.VMEM((tm + HALO, D), _BF),
            pltpu.VMEM((tm // rb, rb + HALO, fc), _F32),
            pltpu.VMEM((tm // rb, rb + HALO, fc), _F32),
        ],
        compiler_params=_params(2),
        name="conv_ffn",
    )(r, r, g_pre, g_post, w_up, w_up, conv_w, conv_w, conv_b, conv_b, w_down)


def _ple_kernel(r_ref, p_ref, win_ref, wgate_ref, ge_ref, gr_ref, out_ref):
    r = r_ref[...]
    e = _rms(_dot(p_ref[...].astype(_BF), win_ref[...]), ge_ref[...])
    gate = _sigmoid(_dot(_rms(r, gr_ref[...]).astype(_BF), wgate_ref[...]))
    out_ref[...] = r + gate * e


def _ple_gate(r, p, w_in, w_gate, g_e, g_r):
    T, D = r.shape
    P = p.shape[1]
    tm = 512
    return pl.pallas_call(
        _ple_kernel,
        grid=(T // tm,),
        in_specs=[
            pl.BlockSpec((tm, D), lambda i: (i, 0)),
            pl.BlockSpec((tm, P), lambda i: (i, 0)),
            _resident((P, D), lambda i: (0, 0)),
            _resident((D, D), lambda i: (0, 0)),
            _resident((1, D), lambda i: (0, 0)),
            _resident((1, D), lambda i: (0, 0)),
        ],
        out_specs=pl.BlockSpec((tm, D), lambda i: (i, 0)),
        out_shape=jax.ShapeDtypeStruct((T, D), _F32),
        compiler_params=_params(1),
        name="ple_gate",
    )(r, p, w_in, w_gate, g_e, g_r)


def kernel(x, p, norm_g, hgrn_w_in, hgrn_lb_logits, hgrn_norm_g, hgrn_w_out, gmlp_w_in, gmlp_ln_g,
           gmlp_ln_b, gmlp_w_s, gmlp_b_s, gmlp_w_out, ffn_w_up, ffn_conv_w, ffn_conv_b, ffn_w_down,
           ple_w_in, ple_w_gate, ple_norm_g):
    B, S, D = x.shape
    depth = p.shape[0]
    H = D // LANES
    T = B * S
    row = lambda a: a.reshape(1, -1)

    lbl = hgrn_lb_logits.astype(_F32).reshape(-1, H, LANES).transpose(1, 0, 2)

    F = ffn_w_down.shape[1]
    ffn_up = ffn_w_up.reshape(depth * D, 2 * F)
    ffn_down = ffn_w_down.reshape(depth * F, D)
    ffn_up_bf = ffn_down_bf = None

    r = x.reshape(T, D)
    for i in range(depth):
        j = i // 2
        if i % 2 == 0:
            to_cast = (ffn_up, ffn_down) if ffn_up_bf is None else ()
            y, cast = _hgrn_mixer(r, row(norm_g[i, 0]), _hgrn_weights(hgrn_w_in[j]), lbl, row(hgrn_norm_g[j]), to_cast,
                                  layer=i, batch=B, seq=S)
            if to_cast:
                ffn_up_bf, ffn_down_bf = cast
            w_out = hgrn_w_out[j]
        else:
            y = _gmlp_mixer(r, row(norm_g[i, 0]), gmlp_w_in[j].astype(_BF), row(gmlp_ln_g[j]),
                            row(gmlp_ln_b[j]), gmlp_w_s[j], gmlp_b_s[j].T)
            w_out = gmlp_w_out[j]
        r = _proj_residual(y, w_out.astype(_BF), r, row(norm_g[i, 1]))
        r = _conv_ffn(r, row(norm_g[i, 2]), row(norm_g[i, 3]), ffn_up_bf, ffn_conv_w[i],
                      row(ffn_conv_b[i]), ffn_down_bf, layer=i, seq=S)
        r = _ple_gate(r, p[i].reshape(T, -1), ple_w_in[i].astype(_BF), ple_w_gate[i].astype(_BF),
                      row(ple_norm_g[i, 0]), row(ple_norm_g[i, 1]))
    return r.reshape(B, S, D)
```

```python
import functools

import jax
import jax.numpy as jnp
from jax import lax
from jax.experimental import pallas as pl
from jax.experimental.pallas import tpu as pltpu

EPS = 1e-6
LANES = 128
HGRN_CHUNK = 64
HGRN_SAFE_DECAY = 60.0
HGRN_HEADS_PER_STEP = 2
GMLP_CHUNK = 128
CONV_W = 3
HALO = 16
NORM_ROWS = 128
VMEM_LIMIT = 60 * 1024 * 1024

_BF = jnp.bfloat16
_F32 = jnp.float32


def _params(n_axes):
    return pltpu.CompilerParams(
        dimension_semantics=("arbitrary",) * n_axes,
        vmem_limit_bytes=VMEM_LIMIT,
    )


def _resident(shape, index_map):
    return pl.BlockSpec(shape, index_map, pipeline_mode=pl.Buffered(1))


def _rms(x, g):
    return x * lax.rsqrt(jnp.mean(x * x, axis=-1, keepdims=True) + EPS) * g


def _sigmoid(x):
    return 1.0 / (1.0 + jnp.exp(-x))


def _dot(a, b):
    return jnp.dot(a, b, preferred_element_type=_F32)


def _dot_nt(a, b):
    return lax.dot_general(a, b, (((1,), (1,)), ((), ())), preferred_element_type=_F32)


def _dot_tn(a, b):
    return lax.dot_general(a, b, (((0,), (0,)), ((), ())), preferred_element_type=_F32)


def _chunk_cumsum(x, pos):
    shift = 1
    while shift < HGRN_CHUNK:
        x = x + jnp.where(pos >= shift, pltpu.roll(x, shift, axis=0), 0.0)
        shift *= 2
    return x


def _hgrn_gates(proj, lb):
    q = proj[:, 0:LANES]
    f = lb + (1.0 - lb) * _sigmoid(proj[:, LANES:2 * LANES])
    v = proj[:, 2 * LANES:3 * LANES]
    og = proj[:, 3 * LANES:4 * LANES]
    pos = lax.broadcasted_iota(jnp.int32, f.shape, 0) % HGRN_CHUNK
    return q, 1.0 - f, v, og, _chunk_cumsum(jnp.log(f), pos)


def _hgrn_out(o, og, ng):
    return (_rms(o, ng) * (og * _sigmoid(og))).astype(_BF)


def _hgrn_wcast_kernel(*refs):
    out_ref = refs[-1]
    for s, w_ref in enumerate(refs[:-1]):
        out_ref[:, s * LANES:(s + 1) * LANES] = w_ref[...].astype(_BF)


def _hgrn_weights(w_in):
    D = w_in.shape[0]
    H = D // LANES
    hp = HGRN_HEADS_PER_STEP
    specs = [pl.BlockSpec((D, LANES), lambda p, hh=hh, g=g: (0, g * H + p * hp + hh))
             for hh in range(hp) for g in range(4)]
    return pl.pallas_call(
        _hgrn_wcast_kernel,
        grid=(H // hp,),
        in_specs=specs,
        out_specs=pl.BlockSpec((None, D, hp * 4 * LANES), lambda p: (p, 0, 0)),
        out_shape=jax.ShapeDtypeStruct((H // hp, D, hp * 4 * LANES), _BF),
        compiler_params=_params(1),
        name="hgrn_weights",
    )(*([w_in] * (4 * hp)))


def _hgrn_kernel(r_ref, gpre_ref, win_ref, lbl_ref, ng_ref, *rest, layer, tm, rb, n_cast):
    cast_in, rest = rest[:n_cast], rest[n_cast:]
    y_ref, cast_out = rest[0], rest[1:1 + n_cast]
    hn_ref, st_ref, proj_ref, o_ref, fq_ref, fb_ref = rest[1 + n_cast:]
    n = pl.program_id(1)
    p = pl.program_id(2)
    heads = range(HGRN_HEADS_PER_STEP)
    head_cols = lambda hh, w: slice(hh * w, (hh + 1) * w)

    for src_ref, dst_ref in zip(cast_in, cast_out):
        dst_ref[...] = src_ref[...].astype(_BF)

    @pl.when(p == 0)
    def _():
        def norm_rows(j, carry):
            rows = pl.ds(pl.multiple_of(j * NORM_ROWS, NORM_ROWS), NORM_ROWS)
            hn_ref[rows, :] = _rms(r_ref[rows, :], gpre_ref[...]).astype(_BF)
            return carry

        lax.fori_loop(0, tm // NORM_ROWS, norm_rows, 0)

    @pl.when(n == 0)
    def _():
        for hh in heads:
            st_ref[p * HGRN_HEADS_PER_STEP + hh] = jnp.zeros((LANES, LANES), _F32)

    lbs = []
    for hh in heads:
        lg = lbl_ref[hh]
        ex = jnp.exp(lg - jnp.max(lg, axis=0, keepdims=True))
        lbs.append(jnp.sum(ex[0:layer + 1], axis=0, keepdims=True) / jnp.sum(ex, axis=0, keepdims=True))

    n_blocks = tm // rb
    for qb in range(n_blocks):
        rows = pl.ds(qb * rb, rb)
        proj_ref[rows, :] = _dot(hn_ref[rows, :], win_ref[...])

    row = lax.broadcasted_iota(jnp.int32, (HGRN_CHUNK, HGRN_CHUNK), 0)
    col = lax.broadcasted_iota(jnp.int32, (HGRN_CHUNK, HGRN_CHUNK), 1)
    causal = row >= col

    sts = [st_ref[p * HGRN_HEADS_PER_STEP + hh] for hh in heads]
    any_unsafe = jnp.zeros((1, LANES), _F32)
    for qb in range(n_blocks):
        rows = pl.ds(qb * rb, rb)
        gates = [_hgrn_gates(proj_ref[rows, head_cols(hh, 4 * LANES)], lbs[hh]) for hh in heads]
        pairs = [(c, hh) for c in range(rb // HGRN_CHUNK) for hh in heads]
        q_mid, k_mid, q_dec, k_dec, vb, d_last = {}, {}, {}, {}, {}, {}
        for c, hh in pairs:
            sl = slice(c * HGRN_CHUNK, (c + 1) * HGRN_CHUNK)
            q, k, v, _, bcum = gates[hh]
            b = bcum[sl]
            qc, kc = q[sl], k[sl]
            b_mid = b[HGRN_CHUNK // 2 - 1:HGRN_CHUNK // 2]
            b_last = b[HGRN_CHUNK - 1:HGRN_CHUNK]
            unsafe = b_last < -HGRN_SAFE_DECAY
            any_unsafe = jnp.maximum(any_unsafe, unsafe.astype(_F32))
            q_mid[c, hh] = jnp.where(unsafe, 0.0, qc * jnp.exp(b - b_mid)).astype(_BF)
            k_mid[c, hh] = jnp.where(unsafe, 0.0, kc * jnp.exp(b_mid - b)).astype(_BF)
            q_dec[c, hh] = (qc * jnp.exp(b)).astype(_BF)
            k_dec[c, hh] = (kc * jnp.exp(b_last - b)).astype(_BF)
            vb[c, hh] = v[sl].astype(_BF)
            d_last[c, hh] = jnp.exp(b_last)
        scores = {ph: _dot_nt(q_mid[ph], k_mid[ph]) for ph in pairs}
        st_add = {ph: _dot_tn(vb[ph], k_dec[ph]) for ph in pairs}
        o_intra = {ph: _dot(jnp.where(causal, scores[ph], 0.0).astype(_BF), vb[ph]) for ph in pairs}
        st_in = {}
        for c, hh in pairs:
            st_in[c, hh] = sts[hh]
            sts[hh] = sts[hh] * d_last[c, hh] + st_add[c, hh]
        o_inter = {ph: _dot_nt(q_dec[ph], st_in[ph].astype(_BF)) for ph in pairs}
        for hh in heads:
            o = jnp.concatenate([o_intra[c, hh] + o_inter[c, hh] for c in range(rb // HGRN_CHUNK)], axis=0)
            o_ref[hh, rows, :] = o
            y_ref[rows, head_cols(hh, LANES)] = _hgrn_out(o, gates[hh][3], ng_ref[...])
    for hh in heads:
        st_ref[p * HGRN_HEADS_PER_STEP + hh] = sts[hh]

    @pl.when(jnp.max(any_unsafe) > 0.0)
    def _():
        s_idx = lax.broadcasted_iota(jnp.int32, (HGRN_CHUNK, 1), 0)

        for hh in heads:
            def chunk_body(c, carry, hh=hh):
                base = pl.multiple_of(c * HGRN_CHUNK, HGRN_CHUNK)
                chunk = pl.ds(base, HGRN_CHUNK)
                q, k, v, og, b = _hgrn_gates(proj_ref[chunk, head_cols(hh, 4 * LANES)], lbs[hh])
                unsafe = b[HGRN_CHUNK - 1:HGRN_CHUNK] < -HGRN_SAFE_DECAY
                fq_ref[...] = jnp.where(unsafe, q, 0.0)
                fb_ref[...] = b

                def row_body(t, carry2):
                    qt = fq_ref[pl.ds(t, 1), :]
                    bt = fb_ref[pl.ds(t, 1), :]
                    w = qt * k * jnp.exp(jnp.minimum(bt - b, 0.0))
                    w = jnp.where(s_idx <= t, w, 0.0)
                    sc = jnp.sum(w, axis=1, keepdims=True)
                    o_ref[hh, pl.ds(base + t, 1), :] += jnp.sum(sc * v, axis=0, keepdims=True)
                    return carry2

                lax.fori_loop(0, HGRN_CHUNK, row_body, 0)
                y_ref[chunk, head_cols(hh, LANES)] = _hgrn_out(o_ref[hh, chunk, :], og, ng_ref[...])
                return carry

            lax.fori_loop(0, tm // HGRN_CHUNK, chunk_body, 0)


def _hgrn_mixer(r, g_pre, w_in_groups, lb_logits_heads, norm_g, to_cast, *, layer, batch, seq):
    T, D = r.shape
    H = D // LANES
    hp = HGRN_HEADS_PER_STEP
    tm = min(1024, seq)
    rb = min(256, tm)
    n_s = seq // tm
    n_steps = batch * n_s * (H // hp)
    step = lambda b, n, p: (b * n_s + n) * (H // hp) + p
    cast_specs = []
    for a in to_cast:
        rows = a.shape[0] // n_steps
        assert rows * n_steps == a.shape[0] and rows % 16 == 0, a.shape
        cast_specs.append(pl.BlockSpec((rows, a.shape[1]), lambda b, n, p: (step(b, n, p), 0)))
    kern = functools.partial(_hgrn_kernel, layer=layer, tm=tm, rb=rb, n_cast=len(to_cast))
    out = pl.pallas_call(
        kern,
        grid=(batch, n_s, H // hp),
        in_specs=[
            pl.BlockSpec((tm, D), lambda b, n, p: (b * n_s + n, 0)),
            _resident((1, D), lambda b, n, p: (0, 0)),
            pl.BlockSpec((None, D, hp * 4 * LANES), lambda b, n, p: (p, 0, 0)),
            pl.BlockSpec((hp,) + lb_logits_heads.shape[1:], lambda b, n, p: (p, 0, 0)),
            _resident((1, LANES), lambda b, n, p: (0, 0)),
        ] + cast_specs,
        out_specs=[pl.BlockSpec((tm, hp * LANES), lambda b, n, p: (b * n_s + n, p))] + cast_specs,
        out_shape=[jax.ShapeDtypeStruct((T, D), _BF)]
        + [jax.ShapeDtypeStruct(a.shape, _BF) for a in to_cast],
        scratch_shapes=[
            pltpu.VMEM((tm, D), _BF),
            pltpu.VMEM((H, LANES, LANES), _F32),
            pltpu.VMEM((tm, hp * 4 * LANES), _F32),
            pltpu.VMEM((hp, tm, LANES), _F32),
            pltpu.VMEM((HGRN_CHUNK, LANES), _F32),
            pltpu.VMEM((HGRN_CHUNK, LANES), _F32),
        ],
        compiler_params=_params(3),
        name="hgrn_mixer",
    )(r, g_pre, w_in_groups, lb_logits_heads, norm_g, *to_cast)
    return out[0], out[1:]


def _gelu_erf(x):
    return 0.5 * x * (1.0 + lax.erf(x * (2.0 ** -0.5)))


def _gmlp_kernel(r_ref, gpre_ref, win_ref, lng_ref, lnb_ref, ws_ref, bst_ref, y_ref, *, rb):
    tm, D = r_ref.shape
    row = lax.broadcasted_iota(jnp.int32, (GMLP_CHUNK, GMLP_CHUNK), 0)
    col = lax.broadcasted_iota(jnp.int32, (GMLP_CHUNK, GMLP_CHUNK), 1)
    causal = row >= col
    groups = range(D // LANES)
    w_s = [jnp.where(causal, ws_ref[g], 0.0).astype(_BF) for g in groups]

    for q in range(tm // rb):
        hn = _rms(r_ref[pl.ds(q * rb, rb), :], gpre_ref[...]).astype(_BF)
        u = _gelu_erf(_dot(hn, win_ref[:, 0:D]))
        v = _gelu_erf(_dot(hn, win_ref[:, D:2 * D]))
        mu = jnp.mean(v, axis=-1, keepdims=True)
        vc = v - mu
        var = jnp.mean(vc * vc, axis=-1, keepdims=True)
        vb = (vc * lax.rsqrt(var + EPS) * lng_ref[...] + lnb_ref[...]).astype(_BF)
        for g in groups:
            cs = slice(g * LANES, (g + 1) * LANES)
            bias = bst_ref[:, g:g + 1]
            for c in range(rb // GMLP_CHUNK):
                rs = slice(c * GMLP_CHUNK, (c + 1) * GMLP_CHUNK)
                mixed = _dot(w_s[g], vb[rs, cs]) + bias
                y_ref[pl.ds(q * rb + c * GMLP_CHUNK, GMLP_CHUNK), cs] = (u[rs, cs] * mixed).astype(_BF)


def _gmlp_mixer(r, g_pre, w_in, ln_g, ln_b, w_s, b_s_t):
    T, D = r.shape
    G = D // LANES
    tm = min(512, T)
    return pl.pallas_call(
        functools.partial(_gmlp_kernel, rb=min(256, tm)),
        grid=(T // tm,),
        in_specs=[
            pl.BlockSpec((tm, D), lambda i: (i, 0)),
            _resident((1, D), lambda i: (0, 0)),
            _resident((D, 2 * D), lambda i: (0, 0)),
            _resident((1, D), lambda i: (0, 0)),
            _resident((1, D), lambda i: (0, 0)),
            _resident((G, GMLP_CHUNK, GMLP_CHUNK), lambda i: (0, 0, 0)),
            _resident((GMLP_CHUNK, G), lambda i: (0, 0)),
        ],
        out_specs=pl.BlockSpec((tm, D), lambda i: (i, 0)),
        out_shape=jax.ShapeDtypeStruct((T, D), _BF),
        compiler_params=_params(1),
        name="gmlp_mixer",
    )(r, g_pre, w_in, ln_g, ln_b, w_s, b_s_t)


def _proj_kernel(y_ref, w_ref, r_ref, g_ref, out_ref, *, rb):
    for q in range(out_ref.shape[0] // rb):
        rows = pl.ds(q * rb, rb)
        out_ref[rows, :] = r_ref[rows, :] + _rms(_dot(y_ref[rows, :], w_ref[...]), g_ref[...])


def _proj_residual(y, w, r, g_post):
    T, D = r.shape
    tm = min(1024, T)
    return pl.pallas_call(
        functools.partial(_proj_kernel, rb=min(256, tm)),
        grid=(T // tm,),
        in_specs=[
            pl.BlockSpec((tm, D), lambda i: (i, 0)),
            _resident((D, D), lambda i: (0, 0)),
            pl.BlockSpec((tm, D), lambda i: (i, 0)),
            _resident((1, D), lambda i: (0, 0)),
        ],
        out_specs=pl.BlockSpec((tm, D), lambda i: (i, 0)),
        out_shape=jax.ShapeDtypeStruct((T, D), _F32),
        compiler_params=_params(1),
        name="proj_residual",
    )(y, w, r, g_post)


def _gelu_tanh(x):
    return 0.5 * x * (1.0 + jnp.tanh((2.0 / jnp.pi) ** 0.5 * (x + 0.044715 * (x * x * x))))


def _ffn_kernel(r_ref, halo_ref, gpre_ref, gpost_ref, wg_ref, wv_ref, cwg_ref, cwv_ref,
                cbg_ref, cbv_ref, wd_ref, out_ref, hn_ref, hg_ref, hv_ref, *, tm, rb, n_chunks, tiles_per_seq):
    i = pl.program_id(0)
    c = pl.program_id(1)
    n_blocks = tm // rb

    def conv(h_ref, q, cw_ref, cb_ref):
        acc = cb_ref[...] + h_ref[q, pl.ds(HALO, rb), :] * cw_ref[CONV_W - 1:CONV_W, :]
        for j in range(1, CONV_W):
            acc = acc + h_ref[q, pl.ds(HALO - j, rb), :] * cw_ref[CONV_W - 1 - j:CONV_W - j, :]
        return acc

    def chunk_step(first, last):
        if first:
            halo = _rms(halo_ref[...], gpre_ref[...])
            halo = jnp.where(i % tiles_per_seq == 0, 0.0, halo)
            hn_ref[pl.ds(0, HALO), :] = halo.astype(_BF)
        for q in range(n_blocks):
            if first:
                hn_ref[pl.ds(HALO + q * rb, rb), :] = _rms(r_ref[pl.ds(q * rb, rb), :], gpre_ref[...]).astype(_BF)
            for h_ref, w_ref in ((hg_ref, wg_ref), (hv_ref, wv_ref)):
                if q == 0:
                    h_ref[0] = _dot(hn_ref[pl.ds(0, HALO + rb), :], w_ref[...])
                else:
                    h_ref[q, pl.ds(HALO, rb), :] = _dot(hn_ref[pl.ds(HALO + q * rb, rb), :], w_ref[...])
                    h_ref[q, pl.ds(HALO - 8, 8), :] = h_ref[q - 1, pl.ds(HALO + rb - 8, 8), :]

        for q in range(n_blocks):
            rows = pl.ds(q * rb, rb)
            gate = conv(hg_ref, q, cwg_ref, cbg_ref)
            val = conv(hv_ref, q, cwv_ref, cbv_ref)
            act = (_gelu_tanh(gate) * val).astype(_BF)
            down = _dot(act, wd_ref[...])
            if not first:
                down = out_ref[rows, :] + down
            if last:
                down = r_ref[rows, :] + _rms(down, gpost_ref[...])
            out_ref[rows, :] = down

    n_f = pl.num_programs(1)
    pl.when(c == 0)(lambda: chunk_step(True, n_chunks == 1))
    if n_chunks > 1:
        pl.when(c == n_f - 1)(lambda: chunk_step(False, True))
    if n_chunks > 2:
        pl.when(jnp.logical_and(c > 0, c < n_f - 1))(lambda: chunk_step(False, False))


def _conv_ffn(r, g_pre, g_post, w_up, conv_w, conv_b, w_down, *, layer, seq):
    T, D = r.shape
    F = w_up.shape[1] // 2
    tm = min(1024, seq)
    rb = min(512, tm)
    fc = min(512, F)
    n_f = F // fc
    kern = functools.partial(_ffn_kernel, tm=tm, rb=rb, n_chunks=n_f, tiles_per_seq=seq // tm)
    return pl.pallas_call(
        kern,
        grid=(T // tm, n_f),
        in_specs=[
            _resident((tm, D), lambda i, c: (i, 0)),
            pl.BlockSpec((HALO, D), lambda i, c: (jnp.maximum(i * (tm // HALO) - 1, 0), 0)),
            _resident((1, D), lambda i, c: (0, 0)),
            _resident((1, D), lambda i, c: (0, 0)),
            pl.BlockSpec((D, fc), lambda i, c: (layer, c)),
            pl.BlockSpec((D, fc), lambda i, c: (layer, c + n_f)),
            pl.BlockSpec((CONV_W, fc), lambda i, c: (0, c)),
            pl.BlockSpec((CONV_W, fc), lambda i, c: (0, c + n_f)),
            pl.BlockSpec((1, fc), lambda i, c: (0, c)),
            pl.BlockSpec((1, fc), lambda i, c: (0, c + n_f)),
            pl.BlockSpec((fc, D), lambda i, c: (layer * n_f + c, 0)),
        ],
        out_specs=pl.BlockSpec((tm, D), lambda i, c: (i, 0)),
        out_shape=jax.ShapeDtypeStruct((T, D), _F32),
        scratch_shapes=[
            pltpu.VMEM((tm + HALO, D), _BF),
            pltpu.VMEM((tm // rb, rb + HALO, fc), _F32),
            pltpu.VMEM((tm // rb, rb + HALO, fc), _F32),
        ],
        compiler_params=_params(2),
        name="conv_ffn",
    )(r, r, g_pre, g_post, w_up, w_up, conv_w, conv_w, conv_b, conv_b, w_down)


def _ple_kernel(r_ref, p_ref, win_ref, wgate_ref, ge_ref, gr_ref, out_ref, *, rb):
    for q in range(out_ref.shape[0] // rb):
        rows = pl.ds(q * rb, rb)
        r = r_ref[rows, :]
        e = _rms(_dot(p_ref[rows, :].astype(_BF), win_ref[...]), ge_ref[...])
        gate = _sigmoid(_dot(_rms(r, gr_ref[...]).astype(_BF), wgate_ref[...]))
        out_ref[rows, :] = r + gate * e


def _ple_gate(r, p, w_in, w_gate, g_e, g_r):
    T, D = r.shape
    P = p.shape[1]
    tm = min(1024, T)
    return pl.pallas_call(
        functools.partial(_ple_kernel, rb=min(256, tm)),
        grid=(T // tm,),
        in_specs=[
            pl.BlockSpec((tm, D), lambda i: (i, 0)),
            pl.BlockSpec((tm, P), lambda i: (i, 0)),
            _resident((P, D), lambda i: (0, 0)),
            _resident((D, D), lambda i: (0, 0)),
            _resident((1, D), lambda i: (0, 0)),
            _resident((1, D), lambda i: (0, 0)),
        ],
        out_specs=pl.BlockSpec((tm, D), lambda i: (i, 0)),
        out_shape=jax.ShapeDtypeStruct((T, D), _F32),
        compiler_params=_params(1),
        name="ple_gate",
    )(r, p, w_in, w_gate, g_e, g_r)


def kernel(x, p, norm_g, hgrn_w_in, hgrn_lb_logits, hgrn_norm_g, hgrn_w_out, gmlp_w_in, gmlp_ln_g,
           gmlp_ln_b, gmlp_w_s, gmlp_b_s, gmlp_w_out, ffn_w_up, ffn_conv_w, ffn_conv_b, ffn_w_down,
           ple_w_in, ple_w_gate, ple_norm_g):
    B, S, D = x.shape
    depth = p.shape[0]
    H = D // LANES
    T = B * S
    row = lambda a: a.reshape(1, -1)

    lbl = hgrn_lb_logits.astype(_F32).reshape(-1, H, LANES).transpose(1, 0, 2)

    F = ffn_w_down.shape[1]
    ffn_up = ffn_w_up.reshape(depth * D, 2 * F)
    ffn_down = ffn_w_down.reshape(depth * F, D)
    ffn_up_bf = ffn_down_bf = None

    r = x.reshape(T, D)
    for i in range(depth):
        j = i // 2
        if i % 2 == 0:
            to_cast = (ffn_up, ffn_down) if ffn_up_bf is None else ()
            y, cast = _hgrn_mixer(r, row(norm_g[i, 0]), _hgrn_weights(hgrn_w_in[j]), lbl, row(hgrn_norm_g[j]), to_cast,
                                  layer=i, batch=B, seq=S)
            if to_cast:
                ffn_up_bf, ffn_down_bf = cast
            w_out = hgrn_w_out[j]
        else:
            y = _gmlp_mixer(r, row(norm_g[i, 0]), gmlp_w_in[j].astype(_BF), row(gmlp_ln_g[j]),
                            row(gmlp_ln_b[j]), gmlp_w_s[j], gmlp_b_s[j].T)
            w_out = gmlp_w_out[j]
        r = _proj_residual(y, w_out.astype(_BF), r, row(norm_g[i, 1]))
        r = _conv_ffn(r, row(norm_g[i, 2]), row(norm_g[i, 3]), ffn_up_bf, ffn_conv_w[i],
                      row(ffn_conv_b[i]), ffn_down_bf, layer=i, seq=S)
        r = _ple_gate(r, p[i].reshape(T, -1), ple_w_in[i].astype(_BF), ple_w_gate[i].astype(_BF),
                      row(ple_norm_g[i, 0]), row(ple_norm_g[i, 1]))
    return r.reshape(B, S, D)
```

```python
import functools

import jax
import jax.numpy as jnp
from jax import lax
from jax.experimental import pallas as pl
from jax.experimental.pallas import tpu as pltpu

EPS = 1e-6
LANES = 128
HGRN_CHUNK = 64
HGRN_SAFE_DECAY = 60.0
HGRN_HEADS_PER_STEP = 2
GMLP_CHUNK = 128
CONV_W = 3
HALO = 16
NORM_ROWS = 128
VMEM_LIMIT = 60 * 1024 * 1024

_BF = jnp.bfloat16
_F32 = jnp.float32


def _params(n_axes):
    return pltpu.CompilerParams(
        dimension_semantics=("arbitrary",) * n_axes,
        vmem_limit_bytes=VMEM_LIMIT,
    )


def _resident(shape, index_map):
    return pl.BlockSpec(shape, index_map, pipeline_mode=pl.Buffered(1))


def _rms(x, g):
    return x * lax.rsqrt(jnp.mean(x * x, axis=-1, keepdims=True) + EPS) * g


def _sigmoid(x):
    return 1.0 / (1.0 + jnp.exp(-x))


def _dot(a, b):
    return jnp.dot(a, b, preferred_element_type=_F32)


def _dot_nt(a, b):
    return lax.dot_general(a, b, (((1,), (1,)), ((), ())), preferred_element_type=_F32)


def _dot_tn(a, b):
    return lax.dot_general(a, b, (((0,), (0,)), ((), ())), preferred_element_type=_F32)


def _chunk_cumsum(x, pos):
    shift = 1
    while shift < HGRN_CHUNK:
        x = x + jnp.where(pos >= shift, pltpu.roll(x, shift, axis=0), 0.0)
        shift *= 2
    return x


def _hgrn_gates(proj, lb):
    q = proj[:, 0:LANES]
    f = lb + (1.0 - lb) * _sigmoid(proj[:, LANES:2 * LANES])
    v = proj[:, 2 * LANES:3 * LANES]
    og = proj[:, 3 * LANES:4 * LANES]
    pos = lax.broadcasted_iota(jnp.int32, f.shape, 0) % HGRN_CHUNK
    return q, 1.0 - f, v, og, _chunk_cumsum(jnp.log(f), pos)


def _hgrn_out(o, og, ng):
    return (_rms(o, ng) * (og * _sigmoid(og))).astype(_BF)


def _hgrn_wcast_kernel(*refs):
    out_ref = refs[-1]
    for s, w_ref in enumerate(refs[:-1]):
        out_ref[:, s * LANES:(s + 1) * LANES] = w_ref[...].astype(_BF)


def _hgrn_weights(w_in):
    D = w_in.shape[0]
    H = D // LANES
    hp = HGRN_HEADS_PER_STEP
    specs = [pl.BlockSpec((D, LANES), lambda p, hh=hh, g=g: (0, g * H + p * hp + hh))
             for hh in range(hp) for g in range(4)]
    return pl.pallas_call(
        _hgrn_wcast_kernel,
        grid=(H // hp,),
        in_specs=specs,
        out_specs=pl.BlockSpec((None, D, hp * 4 * LANES), lambda p: (p, 0, 0)),
        out_shape=jax.ShapeDtypeStruct((H // hp, D, hp * 4 * LANES), _BF),
        compiler_params=_params(1),
        name="hgrn_weights",
    )(*([w_in] * (4 * hp)))


def _hgrn_kernel(r_ref, gpre_ref, win_ref, lbl_ref, ng_ref, *rest, layer, tm, rb, n_cast):
    cast_in, rest = rest[:n_cast], rest[n_cast:]
    y_ref, cast_out = rest[0], rest[1:1 + n_cast]
    hn_ref, st_ref, proj_ref, o_ref, fq_ref, fb_ref = rest[1 + n_cast:]
    n = pl.program_id(1)
    p = pl.program_id(2)
    heads = range(HGRN_HEADS_PER_STEP)
    head_cols = lambda hh, w: slice(hh * w, (hh + 1) * w)

    for src_ref, dst_ref in zip(cast_in, cast_out):
        dst_ref[...] = src_ref[...].astype(_BF)

    @pl.when(p == 0)
    def _():
        def norm_rows(j, carry):
            rows = pl.ds(pl.multiple_of(j * NORM_ROWS, NORM_ROWS), NORM_ROWS)
            hn_ref[rows, :] = _rms(r_ref[rows, :], gpre_ref[...]).astype(_BF)
            return carry

        lax.fori_loop(0, tm // NORM_ROWS, norm_rows, 0)

    @pl.when(n == 0)
    def _():
        for hh in heads:
            st_ref[p * HGRN_HEADS_PER_STEP + hh] = jnp.zeros((LANES, LANES), _F32)

    lbs = []
    for hh in heads:
        lg = lbl_ref[hh]
        ex = jnp.exp(lg - jnp.max(lg, axis=0, keepdims=True))
        lbs.append(jnp.sum(ex[0:layer + 1], axis=0, keepdims=True) / jnp.sum(ex, axis=0, keepdims=True))

    n_blocks = tm // rb
    for qb in range(n_blocks):
        rows = pl.ds(qb * rb, rb)
        proj_ref[rows, :] = _dot(hn_ref[rows, :], win_ref[...])

    row = lax.broadcasted_iota(jnp.int32, (HGRN_CHUNK, HGRN_CHUNK), 0)
    col = lax.broadcasted_iota(jnp.int32, (HGRN_CHUNK, HGRN_CHUNK), 1)
    causal = row >= col

    sts = [st_ref[p * HGRN_HEADS_PER_STEP + hh] for hh in heads]
    any_unsafe = jnp.zeros((1, LANES), _F32)
    for qb in range(n_blocks):
        rows = pl.ds(qb * rb, rb)
        gates = [_hgrn_gates(proj_ref[rows, head_cols(hh, 4 * LANES)], lbs[hh]) for hh in heads]
        pairs = [(c, hh) for c in range(rb // HGRN_CHUNK) for hh in heads]
        q_mid, k_mid, q_dec, k_dec, vb, d_last = {}, {}, {}, {}, {}, {}
        for c, hh in pairs:
            sl = slice(c * HGRN_CHUNK, (c + 1) * HGRN_CHUNK)
            q, k, v, _, bcum = gates[hh]
            b = bcum[sl]
            qc, kc = q[sl], k[sl]
            b_mid = b[HGRN_CHUNK // 2 - 1:HGRN_CHUNK // 2]
            b_last = b[HGRN_CHUNK - 1:HGRN_CHUNK]
            unsafe = b_last < -HGRN_SAFE_DECAY
            any_unsafe = jnp.maximum(any_unsafe, unsafe.astype(_F32))
            q_mid[c, hh] = jnp.where(unsafe, 0.0, qc * jnp.exp(b - b_mid)).astype(_BF)
            k_mid[c, hh] = jnp.where(unsafe, 0.0, kc * jnp.exp(b_mid - b)).astype(_BF)
            q_dec[c, hh] = (qc * jnp.exp(b)).astype(_BF)
            k_dec[c, hh] = (kc * jnp.exp(b_last - b)).astype(_BF)
            vb[c, hh] = v[sl].astype(_BF)
            d_last[c, hh] = jnp.exp(b_last)
        scores = {ph: _dot_nt(q_mid[ph], k_mid[ph]) for ph in pairs}
        st_add = {ph: _dot_tn(vb[ph], k_dec[ph]) for ph in pairs}
        o_intra = {ph: _dot(jnp.where(causal, scores[ph], 0.0).astype(_BF), vb[ph]) for ph in pairs}
        st_in = {}
        for c, hh in pairs:
            st_in[c, hh] = sts[hh]
            sts[hh] = sts[hh] * d_last[c, hh] + st_add[c, hh]
        o_inter = {ph: _dot_nt(q_dec[ph], st_in[ph].astype(_BF)) for ph in pairs}
        for hh in heads:
            o = jnp.concatenate([o_intra[c, hh] + o_inter[c, hh] for c in range(rb // HGRN_CHUNK)], axis=0)
            o_ref[hh, rows, :] = o
            y_ref[rows, head_cols(hh, LANES)] = _hgrn_out(o, gates[hh][3], ng_ref[...])
    for hh in heads:
        st_ref[p * HGRN_HEADS_PER_STEP + hh] = sts[hh]

    @pl.when(jnp.max(any_unsafe) > 0.0)
    def _():
        s_idx = lax.broadcasted_iota(jnp.int32, (HGRN_CHUNK, 1), 0)

        for hh in heads:
            def chunk_body(c, carry, hh=hh):
                base = pl.multiple_of(c * HGRN_CHUNK, HGRN_CHUNK)
                chunk = pl.ds(base, HGRN_CHUNK)
                q, k, v, og, b = _hgrn_gates(proj_ref[chunk, head_cols(hh, 4 * LANES)], lbs[hh])
                unsafe = b[HGRN_CHUNK - 1:HGRN_CHUNK] < -HGRN_SAFE_DECAY
                fq_ref[...] = jnp.where(unsafe, q, 0.0)
                fb_ref[...] = b

                def row_body(t, carry2):
                    qt = fq_ref[pl.ds(t, 1), :]
                    bt = fb_ref[pl.ds(t, 1), :]
                    w = qt * k * jnp.exp(jnp.minimum(bt - b, 0.0))
                    w = jnp.where(s_idx <= t, w, 0.0)
                    sc = jnp.sum(w, axis=1, keepdims=True)
                    o_ref[hh, pl.ds(base + t, 1), :] += jnp.sum(sc * v, axis=0, keepdims=True)
                    return carry2

                lax.fori_loop(0, HGRN_CHUNK, row_body, 0)
                y_ref[chunk, head_cols(hh, LANES)] = _hgrn_out(o_ref[hh, chunk, :], og, ng_ref[...])
                return carry

            lax.fori_loop(0, tm // HGRN_CHUNK, chunk_body, 0)


def _hgrn_mixer(r, g_pre, w_in_groups, lb_logits_heads, norm_g, to_cast, *, layer, batch, seq):
    T, D = r.shape
    H = D // LANES
    hp = HGRN_HEADS_PER_STEP
    tm = min(1024, seq)
    rb = min(512, tm)
    n_s = seq // tm
    n_steps = batch * n_s * (H // hp)
    step = lambda b, n, p: (b * n_s + n) * (H // hp) + p
    cast_specs = []
    for a in to_cast:
        rows = a.shape[0] // n_steps
        assert rows * n_steps == a.shape[0] and rows % 16 == 0, a.shape
        cast_specs.append(pl.BlockSpec((rows, a.shape[1]), lambda b, n, p: (step(b, n, p), 0)))
    kern = functools.partial(_hgrn_kernel, layer=layer, tm=tm, rb=rb, n_cast=len(to_cast))
    out = pl.pallas_call(
        kern,
        grid=(batch, n_s, H // hp),
        in_specs=[
            pl.BlockSpec((tm, D), lambda b, n, p: (b * n_s + n, 0)),
            _resident((1, D), lambda b, n, p: (0, 0)),
            pl.BlockSpec((None, D, hp * 4 * LANES), lambda b, n, p: (p, 0, 0)),
            pl.BlockSpec((hp,) + lb_logits_heads.shape[1:], lambda b, n, p: (p, 0, 0)),
            _resident((1, LANES), lambda b, n, p: (0, 0)),
        ] + cast_specs,
        out_specs=[pl.BlockSpec((tm, hp * LANES), lambda b, n, p: (b * n_s + n, p))] + cast_specs,
        out_shape=[jax.ShapeDtypeStruct((T, D), _BF)]
        + [jax.ShapeDtypeStruct(a.shape, _BF) for a in to_cast],
        scratch_shapes=[
            pltpu.VMEM((tm, D), _BF),
            pltpu.VMEM((H, LANES, LANES), _F32),
            pltpu.VMEM((tm, hp * 4 * LANES), _F32),
            pltpu.VMEM((hp, tm, LANES), _F32),
            pltpu.VMEM((HGRN_CHUNK, LANES), _F32),
            pltpu.VMEM((HGRN_CHUNK, LANES), _F32),
        ],
        compiler_params=_params(3),
        name="hgrn_mixer",
    )(r, g_pre, w_in_groups, lb_logits_heads, norm_g, *to_cast)
    return out[0], out[1:]


def _gelu_erf(x):
    return 0.5 * x * (1.0 + lax.erf(x * (2.0 ** -0.5)))


def _gmlp_kernel(r_ref, gpre_ref, win_ref, lng_ref, lnb_ref, ws_ref, bst_ref, y_ref, *, rb):
    tm, D = r_ref.shape
    row = lax.broadcasted_iota(jnp.int32, (GMLP_CHUNK, GMLP_CHUNK), 0)
    col = lax.broadcasted_iota(jnp.int32, (GMLP_CHUNK, GMLP_CHUNK), 1)
    causal = row >= col
    groups = range(D // LANES)
    w_s = [jnp.where(causal, ws_ref[g], 0.0).astype(_BF) for g in groups]

    for q in range(tm // rb):
        hn = _rms(r_ref[pl.ds(q * rb, rb), :], gpre_ref[...]).astype(_BF)
        u = _gelu_erf(_dot(hn, win_ref[:, 0:D]))
        v = _gelu_erf(_dot(hn, win_ref[:, D:2 * D]))
        mu = jnp.mean(v, axis=-1, keepdims=True)
        vc = v - mu
        var = jnp.mean(vc * vc, axis=-1, keepdims=True)
        vb = (vc * lax.rsqrt(var + EPS) * lng_ref[...] + lnb_ref[...]).astype(_BF)
        for g in groups:
            cs = slice(g * LANES, (g + 1) * LANES)
            bias = bst_ref[:, g:g + 1]
            for c in range(rb // GMLP_CHUNK):
                rs = slice(c * GMLP_CHUNK, (c + 1) * GMLP_CHUNK)
                mixed = _dot(w_s[g], vb[rs, cs]) + bias
                y_ref[pl.ds(q * rb + c * GMLP_CHUNK, GMLP_CHUNK), cs] = (u[rs, cs] * mixed).astype(_BF)


def _gmlp_mixer(r, g_pre, w_in, ln_g, ln_b, w_s, b_s_t):
    T, D = r.shape
    G = D // LANES
    tm = min(512, T)
    return pl.pallas_call(
        functools.partial(_gmlp_kernel, rb=min(256, tm)),
        grid=(T // tm,),
        in_specs=[
            pl.BlockSpec((tm, D), lambda i: (i, 0)),
            _resident((1, D), lambda i: (0, 0)),
            _resident((D, 2 * D), lambda i: (0, 0)),
            _resident((1, D), lambda i: (0, 0)),
            _resident((1, D), lambda i: (0, 0)),
            _resident((G, GMLP_CHUNK, GMLP_CHUNK), lambda i: (0, 0, 0)),
            _resident((GMLP_CHUNK, G), lambda i: (0, 0)),
        ],
        out_specs=pl.BlockSpec((tm, D), lambda i: (i, 0)),
        out_shape=jax.ShapeDtypeStruct((T, D), _BF),
        compiler_params=_params(1),
        name="gmlp_mixer",
    )(r, g_pre, w_in, ln_g, ln_b, w_s, b_s_t)


def _proj_kernel(y_ref, w_ref, r_ref, g_ref, out_ref):
    out_ref[...] = r_ref[...] + _rms(_dot(y_ref[...], w_ref[...]), g_ref[...])


def _proj_residual(y, w, r, g_post):
    T, D = r.shape
    tm = 512
    return pl.pallas_call(
        _proj_kernel,
        grid=(T // tm,),
        in_specs=[
            pl.BlockSpec((tm, D), lambda i: (i, 0)),
            _resident((D, D), lambda i: (0, 0)),
            pl.BlockSpec((tm, D), lambda i: (i, 0)),
            _resident((1, D), lambda i: (0, 0)),
        ],
        out_specs=pl.BlockSpec((tm, D), lambda i: (i, 0)),
        out_shape=jax.ShapeDtypeStruct((T, D), _F32),
        compiler_params=_params(1),
        name="proj_residual",
    )(y, w, r, g_post)


def _gelu_tanh(x):
    return 0.5 * x * (1.0 + jnp.tanh((2.0 / jnp.pi) ** 0.5 * (x + 0.044715 * (x * x * x))))


def _ffn_kernel(r_ref, halo_ref, gpre_ref, gpost_ref, wg_ref, wv_ref, cwg_ref, cwv_ref,
                cbg_ref, cbv_ref, wd_ref, out_ref, hn_ref, hg_ref, hv_ref, *, tm, rb, n_chunks, tiles_per_seq):
    i = pl.program_id(0)
    c = pl.program_id(1)
    n_blocks = tm // rb

    def conv(h_ref, q, cw_ref, cb_ref):
        acc = cb_ref[...] + h_ref[q, pl.ds(HALO, rb), :] * cw_ref[CONV_W - 1:CONV_W, :]
        for j in range(1, CONV_W):
            acc = acc + h_ref[q, pl.ds(HALO - j, rb), :] * cw_ref[CONV_W - 1 - j:CONV_W - j, :]
        return acc

    def chunk_step(first, last):
        if first:
            halo = _rms(halo_ref[...], gpre_ref[...])
            halo = jnp.where(i % tiles_per_seq == 0, 0.0, halo)
            hn_ref[pl.ds(0, HALO), :] = halo.astype(_BF)
        for q in range(n_blocks):
            if first:
                hn_ref[pl.ds(HALO + q * rb, rb), :] = _rms(r_ref[pl.ds(q * rb, rb), :], gpre_ref[...]).astype(_BF)
            for h_ref, w_ref in ((hg_ref, wg_ref), (hv_ref, wv_ref)):
                if q == 0:
                    h_ref[0] = _dot(hn_ref[pl.ds(0, HALO + rb), :], w_ref[...])
                else:
                    h_ref[q, pl.ds(HALO, rb), :] = _dot(hn_ref[pl.ds(HALO + q * rb, rb), :], w_ref[...])
                    h_ref[q, pl.ds(HALO - 8, 8), :] = h_ref[q - 1, pl.ds(HALO + rb - 8, 8), :]

        for q in range(n_blocks):
            rows = pl.ds(q * rb, rb)
            gate = conv(hg_ref, q, cwg_ref, cbg_ref)
            val = conv(hv_ref, q, cwv_ref, cbv_ref)
            act = (_gelu_tanh(gate) * val).astype(_BF)
            down = _dot(act, wd_ref[...])
            if not first:
                down = out_ref[rows, :] + down
            if last:
                down = r_ref[rows, :] + _rms(down, gpost_ref[...])
            out_ref[rows, :] = down

    n_f = pl.num_programs(1)
    pl.when(c == 0)(lambda: chunk_step(True, n_chunks == 1))
    if n_chunks > 1:
        pl.when(c == n_f - 1)(lambda: chunk_step(False, True))
    if n_chunks > 2:
        pl.when(jnp.logical_and(c > 0, c < n_f - 1))(lambda: chunk_step(False, False))


def _conv_ffn(r, g_pre, g_post, w_up, conv_w, conv_b, w_down, *, layer, seq):
    T, D = r.shape
    F = w_up.shape[1] // 2
    tm = min(1024, seq)
    rb = min(512, tm)
    fc = min(512, F)
    n_f = F // fc
    kern = functools.partial(_ffn_kernel, tm=tm, rb=rb, n_chunks=n_f, tiles_per_seq=seq // tm)
    return pl.pallas_call(
        kern,
        grid=(T // tm, n_f),
        in_specs=[
            _resident((tm, D), lambda i, c: (i, 0)),
            pl.BlockSpec((HALO, D), lambda i, c: (jnp.maximum(i * (tm // HALO) - 1, 0), 0)),
            _resident((1, D), lambda i, c: (0, 0)),
            _resident((1, D), lambda i, c: (0, 0)),
            pl.BlockSpec((D, fc), lambda i, c: (layer, c)),
            pl.BlockSpec((D, fc), lambda i, c: (layer, c + n_f)),
            pl.BlockSpec((CONV_W, fc), lambda i, c: (0, c)),
            pl.BlockSpec((CONV_W, fc), lambda i, c: (0, c + n_f)),
            pl.BlockSpec((1, fc), lambda i, c: (0, c)),
            pl.BlockSpec((1, fc), lambda i, c: (0, c + n_f)),
            pl.BlockSpec((fc, D), lambda i, c: (layer * n_f + c, 0)),
        ],
        out_specs=pl.BlockSpec((tm, D), lambda i, c: (i, 0)),
        out_shape=jax.ShapeDtypeStruct((T, D), _F32),
        scratch_shapes=[
            pltpu.VMEM((tm + HALO, D), _BF),
            pltpu.VMEM((tm // rb, rb + HALO, fc), _F32),
            pltpu.VMEM((tm // rb, rb + HALO, fc), _F32),
        ],
        compiler_params=_params(2),
        name="conv_ffn",
    )(r, r, g_pre, g_post, w_up, w_up, conv_w, conv_w, conv_b, conv_b, w_down)


def _ple_kernel(r_ref, p_ref, win_ref, wgate_ref, ge_ref, gr_ref, out_ref):
    r = r_ref[...]
    e = _rms(_dot(p_ref[...].astype(_BF), win_ref[...]), ge_ref[...])
    gate = _sigmoid(_dot(_rms(r, gr_ref[...]).astype(_BF), wgate_ref[...]))
    out_ref[...] = r + gate * e


def _ple_gate(r, p, w_in, w_gate, g_e, g_r):
    T, D = r.shape
    P = p.shape[1]
    tm = 512
    return pl.pallas_call(
        _ple_kernel,
        grid=(T // tm,),
        in_specs=[
            pl.BlockSpec((tm, D), lambda i: (i, 0)),
            pl.BlockSpec((tm, P), lambda i: (i, 0)),
            _resident((P, D), lambda i: (0, 0)),
            _resident((D, D), lambda i: (0, 0)),
            _resident((1, D), lambda i: (0, 0)),
            _resident((1, D), lambda i: (0, 0)),
        ],
        out_specs=pl.BlockSpec((tm, D), lambda i: (i, 0)),
        out_shape=jax.ShapeDtypeStruct((T, D), _F32),
        compiler_params=_params(1),
        name="ple_gate",
    )(r, p, w_in, w_gate, g_e, g_r)


def kernel(x, p, norm_g, hgrn_w_in, hgrn_lb_logits, hgrn_norm_g, hgrn_w_out, gmlp_w_in, gmlp_ln_g,
           gmlp_ln_b, gmlp_w_s, gmlp_b_s, gmlp_w_out, ffn_w_up, ffn_conv_w, ffn_conv_b, ffn_w_down,
           ple_w_in, ple_w_gate, ple_norm_g):
    B, S, D = x.shape
    depth = p.shape[0]
    H = D // LANES
    T = B * S
    row = lambda a: a.reshape(1, -1)

    lbl = hgrn_lb_logits.astype(_F32).reshape(-1, H, LANES).transpose(1, 0, 2)

    F = ffn_w_down.shape[1]
    ffn_up = ffn_w_up.reshape(depth * D, 2 * F)
    ffn_down = ffn_w_down.reshape(depth * F, D)
    ffn_up_bf = ffn_down_bf = None

    r = x.reshape(T, D)
    for i in range(depth):
        j = i // 2
        if i % 2 == 0:
            to_cast = (ffn_up, ffn_down) if ffn_up_bf is None else ()
            y, cast = _hgrn_mixer(r, row(norm_g[i, 0]), _hgrn_weights(hgrn_w_in[j]), lbl, row(hgrn_norm_g[j]), to_cast,
                                  layer=i, batch=B, seq=S)
            if to_cast:
                ffn_up_bf, ffn_down_bf = cast
            w_out = hgrn_w_out[j]
        else:
            y = _gmlp_mixer(r, row(norm_g[i, 0]), gmlp_w_in[j].astype(_BF), row(gmlp_ln_g[j]),
                            row(gmlp_ln_b[j]), gmlp_w_s[j], gmlp_b_s[j].T)
            w_out = gmlp_w_out[j]
        r = _proj_residual(y, w_out.astype(_BF), r, row(norm_g[i, 1]))
        r = _conv_ffn(r, row(norm_g[i, 2]), row(norm_g[i, 3]), ffn_up_bf, ffn_conv_w[i],
                      row(ffn_conv_b[i]), ffn_down_bf, layer=i, seq=S)
        r = _ple_gate(r, p[i].reshape(T, -1), ple_w_in[i].astype(_BF), ple_w_gate[i].astype(_BF),
                      row(ple_norm_g[i, 0]), row(ple_norm_g[i, 1]))
    return r.reshape(B, S, D)
```

```python
import functools

import jax
import jax.numpy as jnp
from jax import lax
from jax.experimental import pallas as pl
from jax.experimental.pallas import tpu as pltpu

EPS = 1e-6
LANES = 128
HGRN_CHUNK = 64
HGRN_SAFE_DECAY = 60.0
HGRN_HEADS_PER_STEP = 2
GMLP_CHUNK = 128
CONV_W = 3
HALO = 16
NORM_ROWS = 128
VMEM_LIMIT = 60 * 1024 * 1024

_BF = jnp.bfloat16
_F32 = jnp.float32


def _params(n_axes):
    return pltpu.CompilerParams(
        dimension_semantics=("arbitrary",) * n_axes,
        vmem_limit_bytes=VMEM_LIMIT,
    )


def _resident(shape, index_map):
    return pl.BlockSpec(shape, index_map, pipeline_mode=pl.Buffered(1))


def _rms(x, g):
    return x * lax.rsqrt(jnp.mean(x * x, axis=-1, keepdims=True) + EPS) * g


def _sigmoid(x):
    return 1.0 / (1.0 + jnp.exp(-x))


def _dot(a, b):
    return jnp.dot(a, b, preferred_element_type=_F32)


def _dot_nt(a, b):
    return lax.dot_general(a, b, (((1,), (1,)), ((), ())), preferred_element_type=_F32)


def _dot_tn(a, b):
    return lax.dot_general(a, b, (((0,), (0,)), ((), ())), preferred_element_type=_F32)


def _chunk_cumsum(x, pos):
    shift = 1
    while shift < HGRN_CHUNK:
        x = x + jnp.where(pos >= shift, pltpu.roll(x, shift, axis=0), 0.0)
        shift *= 2
    return x


def _hgrn_gates(proj, lb):
    q = proj[:, 0:LANES]
    f = lb + (1.0 - lb) * _sigmoid(proj[:, LANES:2 * LANES])
    v = proj[:, 2 * LANES:3 * LANES]
    og = proj[:, 3 * LANES:4 * LANES]
    pos = lax.broadcasted_iota(jnp.int32, f.shape, 0) % HGRN_CHUNK
    return q, 1.0 - f, v, og, _chunk_cumsum(jnp.log(f), pos)


def _hgrn_out(o, og, ng):
    return (_rms(o, ng) * (og * _sigmoid(og))).astype(_BF)


def _hgrn_wcast_kernel(*refs):
    out_ref = refs[-1]
    for s, w_ref in enumerate(refs[:-1]):
        out_ref[:, s * LANES:(s + 1) * LANES] = w_ref[...].astype(_BF)


def _hgrn_weights(w_in):
    D = w_in.shape[0]
    H = D // LANES
    hp = HGRN_HEADS_PER_STEP
    specs = [pl.BlockSpec((D, LANES), lambda p, hh=hh, g=g: (0, g * H + p * hp + hh))
             for hh in range(hp) for g in range(4)]
    return pl.pallas_call(
        _hgrn_wcast_kernel,
        grid=(H // hp,),
        in_specs=specs,
        out_specs=pl.BlockSpec((None, D, hp * 4 * LANES), lambda p: (p, 0, 0)),
        out_shape=jax.ShapeDtypeStruct((H // hp, D, hp * 4 * LANES), _BF),
        compiler_params=_params(1),
        name="hgrn_weights",
    )(*([w_in] * (4 * hp)))


def _hgrn_kernel(r_ref, gpre_ref, win_ref, lbl_ref, ng_ref, *rest, layer, tm, rb, n_cast):
    cast_in, rest = rest[:n_cast], rest[n_cast:]
    y_ref, cast_out = rest[0], rest[1:1 + n_cast]
    hn_ref, st_ref, proj_ref, o_ref, fq_ref, fb_ref = rest[1 + n_cast:]
    n = pl.program_id(1)
    p = pl.program_id(2)
    heads = range(HGRN_HEADS_PER_STEP)
    head_cols = lambda hh, w: slice(hh * w, (hh + 1) * w)

    for src_ref, dst_ref in zip(cast_in, cast_out):
        dst_ref[...] = src_ref[...].astype(_BF)

    @pl.when(p == 0)
    def _():
        def norm_rows(j, carry):
            rows = pl.ds(pl.multiple_of(j * NORM_ROWS, NORM_ROWS), NORM_ROWS)
            hn_ref[rows, :] = _rms(r_ref[rows, :], gpre_ref[...]).astype(_BF)
            return carry

        lax.fori_loop(0, tm // NORM_ROWS, norm_rows, 0)

    @pl.when(n == 0)
    def _():
        for hh in heads:
            st_ref[p * HGRN_HEADS_PER_STEP + hh] = jnp.zeros((LANES, LANES), _F32)

    lbs = []
    for hh in heads:
        lg = lbl_ref[hh]
        ex = jnp.exp(lg - jnp.max(lg, axis=0, keepdims=True))
        lbs.append(jnp.sum(ex[0:layer + 1], axis=0, keepdims=True) / jnp.sum(ex, axis=0, keepdims=True))

    n_blocks = tm // rb
    for qb in range(n_blocks):
        rows = pl.ds(qb * rb, rb)
        proj_ref[rows, :] = _dot(hn_ref[rows, :], win_ref[...])

    row = lax.broadcasted_iota(jnp.int32, (HGRN_CHUNK, HGRN_CHUNK), 0)
    col = lax.broadcasted_iota(jnp.int32, (HGRN_CHUNK, HGRN_CHUNK), 1)
    causal = row >= col

    sts = [st_ref[p * HGRN_HEADS_PER_STEP + hh] for hh in heads]
    any_unsafe = jnp.zeros((1, LANES), _F32)
    for qb in range(n_blocks):
        rows = pl.ds(qb * rb, rb)
        gates = [_hgrn_gates(proj_ref[rows, head_cols(hh, 4 * LANES)], lbs[hh]) for hh in heads]
        pairs = [(c, hh) for c in range(rb // HGRN_CHUNK) for hh in heads]
        q_mid, k_mid, q_dec, k_dec, vb, d_last = {}, {}, {}, {}, {}, {}
        for c, hh in pairs:
            sl = slice(c * HGRN_CHUNK, (c + 1) * HGRN_CHUNK)
            q, k, v, _, bcum = gates[hh]
            b = bcum[sl]
            qc, kc = q[sl], k[sl]
            b_mid = b[HGRN_CHUNK // 2 - 1:HGRN_CHUNK // 2]
            b_last = b[HGRN_CHUNK - 1:HGRN_CHUNK]
            unsafe = b_last < -HGRN_SAFE_DECAY
            any_unsafe = jnp.maximum(any_unsafe, unsafe.astype(_F32))
            q_mid[c, hh] = jnp.where(unsafe, 0.0, qc * jnp.exp(b - b_mid)).astype(_BF)
            k_mid[c, hh] = jnp.where(unsafe, 0.0, kc * jnp.exp(b_mid - b)).astype(_BF)
            q_dec[c, hh] = (qc * jnp.exp(b)).astype(_BF)
            k_dec[c, hh] = (kc * jnp.exp(b_last - b)).astype(_BF)
            vb[c, hh] = v[sl].astype(_BF)
            d_last[c, hh] = jnp.exp(b_last)
        scores = {ph: _dot_nt(q_mid[ph], k_mid[ph]) for ph in pairs}
        st_add = {ph: _dot_tn(vb[ph], k_dec[ph]) for ph in pairs}
        o_intra = {ph: _dot(jnp.where(causal, scores[ph], 0.0).astype(_BF), vb[ph]) for ph in pairs}
        st_in = {}
        for c, hh in pairs:
            st_in[c, hh] = sts[hh]
            sts[hh] = sts[hh] * d_last[c, hh] + st_add[c, hh]
        o_inter = {ph: _dot_nt(q_dec[ph], st_in[ph].astype(_BF)) for ph in pairs}
        for hh in heads:
            o = jnp.concatenate([o_intra[c, hh] + o_inter[c, hh] for c in range(rb // HGRN_CHUNK)], axis=0)
            o_ref[hh, rows, :] = o
            y_ref[rows, head_cols(hh, LANES)] = _hgrn_out(o, gates[hh][3], ng_ref[...])
    for hh in heads:
        st_ref[p * HGRN_HEADS_PER_STEP + hh] = sts[hh]

    @pl.when(jnp.max(any_unsafe) > 0.0)
    def _():
        s_idx = lax.broadcasted_iota(jnp.int32, (HGRN_CHUNK, 1), 0)

        for hh in heads:
            def chunk_body(c, carry, hh=hh):
                base = pl.multiple_of(c * HGRN_CHUNK, HGRN_CHUNK)
                chunk = pl.ds(base, HGRN_CHUNK)
                q, k, v, og, b = _hgrn_gates(proj_ref[chunk, head_cols(hh, 4 * LANES)], lbs[hh])
                unsafe = b[HGRN_CHUNK - 1:HGRN_CHUNK] < -HGRN_SAFE_DECAY
                fq_ref[...] = jnp.where(unsafe, q, 0.0)
                fb_ref[...] = b

                def row_body(t, carry2):
                    qt = fq_ref[pl.ds(t, 1), :]
                    bt = fb_ref[pl.ds(t, 1), :]
                    w = qt * k * jnp.exp(jnp.minimum(bt - b, 0.0))
                    w = jnp.where(s_idx <= t, w, 0.0)
                    sc = jnp.sum(w, axis=1, keepdims=True)
                    o_ref[hh, pl.ds(base + t, 1), :] += jnp.sum(sc * v, axis=0, keepdims=True)
                    return carry2

                lax.fori_loop(0, HGRN_CHUNK, row_body, 0)
                y_ref[chunk, head_cols(hh, LANES)] = _hgrn_out(o_ref[hh, chunk, :], og, ng_ref[...])
                return carry

            lax.fori_loop(0, tm // HGRN_CHUNK, chunk_body, 0)


def _hgrn_mixer(r, g_pre, w_in_groups, lb_logits_heads, norm_g, to_cast, *, layer, batch, seq):
    T, D = r.shape
    H = D // LANES
    hp = HGRN_HEADS_PER_STEP
    tm = min(1024, seq)
    rb = min(512, tm)
    n_s = seq // tm
    n_steps = batch * n_s * (H // hp)
    step = lambda b, n, p: (b * n_s + n) * (H // hp) + p
    cast_specs = []
    for a in to_cast:
        rows = a.shape[0] // n_steps
        assert rows * n_steps == a.shape[0] and rows % 16 == 0, a.shape
        cast_specs.append(pl.BlockSpec((rows, a.shape[1]), lambda b, n, p: (step(b, n, p), 0)))
    kern = functools.partial(_hgrn_kernel, layer=layer, tm=tm, rb=rb, n_cast=len(to_cast))
    out = pl.pallas_call(
        kern,
        grid=(batch, n_s, H // hp),
        in_specs=[
            pl.BlockSpec((tm, D), lambda b, n, p: (b * n_s + n, 0)),
            _resident((1, D), lambda b, n, p: (0, 0)),
            pl.BlockSpec((None, D, hp * 4 * LANES), lambda b, n, p: (p, 0, 0)),
            pl.BlockSpec((hp,) + lb_logits_heads.shape[1:], lambda b, n, p: (p, 0, 0)),
            _resident((1, LANES), lambda b, n, p: (0, 0)),
        ] + cast_specs,
        out_specs=[pl.BlockSpec((tm, hp * LANES), lambda b, n, p: (b * n_s + n, p))] + cast_specs,
        out_shape=[jax.ShapeDtypeStruct((T, D), _BF)]
        + [jax.ShapeDtypeStruct(a.shape, _BF) for a in to_cast],
        scratch_shapes=[
            pltpu.VMEM((tm, D), _BF),
            pltpu.VMEM((H, LANES, LANES), _F32),
            pltpu.VMEM((tm, hp * 4 * LANES), _F32),
            pltpu.VMEM((hp, tm, LANES), _F32),
            pltpu.VMEM((HGRN_CHUNK, LANES), _F32),
            pltpu.VMEM((HGRN_CHUNK, LANES), _F32),
        ],
        compiler_params=_params(3),
        name="hgrn_mixer",
    )(r, g_pre, w_in_groups, lb_logits_heads, norm_g, *to_cast)
    return out[0], out[1:]


def _gelu_erf(x):
    return 0.5 * x * (1.0 + lax.erf(x * (2.0 ** -0.5)))


def _gmlp_kernel(r_ref, gpre_ref, win_ref, lng_ref, lnb_ref, ws_ref, bst_ref, y_ref, *, rb):
    tm, D = r_ref.shape
    row = lax.broadcasted_iota(jnp.int32, (GMLP_CHUNK, GMLP_CHUNK), 0)
    col = lax.broadcasted_iota(jnp.int32, (GMLP_CHUNK, GMLP_CHUNK), 1)
    causal = row >= col
    groups = range(D // LANES)
    w_s = [jnp.where(causal, ws_ref[g], 0.0).astype(_BF) for g in groups]

    for q in range(tm // rb):
        hn = _rms(r_ref[pl.ds(q * rb, rb), :], gpre_ref[...]).astype(_BF)
        u = _gelu_erf(_dot(hn, win_ref[:, 0:D]))
        v = _gelu_erf(_dot(hn, win_ref[:, D:2 * D]))
        mu = jnp.mean(v, axis=-1, keepdims=True)
        vc = v - mu
        var = jnp.mean(vc * vc, axis=-1, keepdims=True)
        vb = (vc * lax.rsqrt(var + EPS) * lng_ref[...] + lnb_ref[...]).astype(_BF)
        for g in groups:
            cs = slice(g * LANES, (g + 1) * LANES)
            bias = bst_ref[:, g:g + 1]
            for c in range(rb // GMLP_CHUNK):
                rs = slice(c * GMLP_CHUNK, (c + 1) * GMLP_CHUNK)
                mixed = _dot(w_s[g], vb[rs, cs]) + bias
                y_ref[pl.ds(q * rb + c * GMLP_CHUNK, GMLP_CHUNK), cs] = (u[rs, cs] * mixed).astype(_BF)


def _gmlp_mixer(r, g_pre, w_in, ln_g, ln_b, w_s, b_s_t):
    T, D = r.shape
    G = D // LANES
    tm = min(512, T)
    return pl.pallas_call(
        functools.partial(_gmlp_kernel, rb=min(512, tm)),
        grid=(T // tm,),
        in_specs=[
            pl.BlockSpec((tm, D), lambda i: (i, 0)),
            _resident((1, D), lambda i: (0, 0)),
            _resident((D, 2 * D), lambda i: (0, 0)),
            _resident((1, D), lambda i: (0, 0)),
            _resident((1, D), lambda i: (0, 0)),
            _resident((G, GMLP_CHUNK, GMLP_CHUNK), lambda i: (0, 0, 0)),
            _resident((GMLP_CHUNK, G), lambda i: (0, 0)),
        ],
        out_specs=pl.BlockSpec((tm, D), lambda i: (i, 0)),
        out_shape=jax.ShapeDtypeStruct((T, D), _BF),
        compiler_params=_params(1),
        name="gmlp_mixer",
    )(r, g_pre, w_in, ln_g, ln_b, w_s, b_s_t)


def _proj_kernel(y_ref, w_ref, r_ref, g_ref, out_ref):
    out_ref[...] = r_ref[...] + _rms(_dot(y_ref[...], w_ref[...]), g_ref[...])


def _proj_residual(y, w, r, g_post):
    T, D = r.shape
    tm = min(1024, T)
    return pl.pallas_call(
        _proj_kernel,
        grid=(T // tm,),
        in_specs=[
            pl.BlockSpec((tm, D), lambda i: (i, 0)),
            _resident((D, D), lambda i: (0, 0)),
            pl.BlockSpec((tm, D), lambda i: (i, 0)),
            _resident((1, D), lambda i: (0, 0)),
        ],
        out_specs=pl.BlockSpec((tm, D), lambda i: (i, 0)),
        out_shape=jax.ShapeDtypeStruct((T, D), _F32),
        compiler_params=_params(1),
        name="proj_residual",
    )(y, w, r, g_post)


def _gelu_tanh(x):
    return 0.5 * x * (1.0 + jnp.tanh((2.0 / jnp.pi) ** 0.5 * (x + 0.044715 * (x * x * x))))


def _ffn_kernel(r_ref, halo_ref, gpre_ref, gpost_ref, wg_ref, wv_ref, cwg_ref, cwv_ref,
                cbg_ref, cbv_ref, wd_ref, out_ref, hn_ref, hg_ref, hv_ref, *, tm, rb, n_chunks, tiles_per_seq):
    i = pl.program_id(0)
    c = pl.program_id(1)
    n_blocks = tm // rb

    def conv(h_ref, q, cw_ref, cb_ref):
        acc = cb_ref[...] + h_ref[q, pl.ds(HALO, rb), :] * cw_ref[CONV_W - 1:CONV_W, :]
        for j in range(1, CONV_W):
            acc = acc + h_ref[q, pl.ds(HALO - j, rb), :] * cw_ref[CONV_W - 1 - j:CONV_W - j, :]
        return acc

    def chunk_step(first, last):
        if first:
            halo = _rms(halo_ref[...], gpre_ref[...])
            halo = jnp.where(i % tiles_per_seq == 0, 0.0, halo)
            hn_ref[pl.ds(0, HALO), :] = halo.astype(_BF)
        for q in range(n_blocks):
            if first:
                hn_ref[pl.ds(HALO + q * rb, rb), :] = _rms(r_ref[pl.ds(q * rb, rb), :], gpre_ref[...]).astype(_BF)
            for h_ref, w_ref in ((hg_ref, wg_ref), (hv_ref, wv_ref)):
                if q == 0:
                    h_ref[0] = _dot(hn_ref[pl.ds(0, HALO + rb), :], w_ref[...])
                else:
                    h_ref[q, pl.ds(HALO, rb), :] = _dot(hn_ref[pl.ds(HALO + q * rb, rb), :], w_ref[...])
                    h_ref[q, pl.ds(HALO - 8, 8), :] = h_ref[q - 1, pl.ds(HALO + rb - 8, 8), :]

        for q in range(n_blocks):
            rows = pl.ds(q * rb, rb)
            gate = conv(hg_ref, q, cwg_ref, cbg_ref)
            val = conv(hv_ref, q, cwv_ref, cbv_ref)
            act = (_gelu_tanh(gate) * val).astype(_BF)
            down = _dot(act, wd_ref[...])
            if not first:
                down = out_ref[rows, :] + down
            if last:
                down = r_ref[rows, :] + _rms(down, gpost_ref[...])
            out_ref[rows, :] = down

    n_f = pl.num_programs(1)
    pl.when(c == 0)(lambda: chunk_step(True, n_chunks == 1))
    if n_chunks > 1:
        pl.when(c == n_f - 1)(lambda: chunk_step(False, True))
    if n_chunks > 2:
        pl.when(jnp.logical_and(c > 0, c < n_f - 1))(lambda: chunk_step(False, False))


def _conv_ffn(r, g_pre, g_post, w_up, conv_w, conv_b, w_down, *, layer, seq):
    T, D = r.shape
    F = w_up.shape[1] // 2
    tm = min(1024, seq)
    rb = min(512, tm)
    fc = min(512, F)
    n_f = F // fc
    kern = functools.partial(_ffn_kernel, tm=tm, rb=rb, n_chunks=n_f, tiles_per_seq=seq // tm)
    return pl.pallas_call(
        kern,
        grid=(T // tm, n_f),
        in_specs=[
            _resident((tm, D), lambda i, c: (i, 0)),
            pl.BlockSpec((HALO, D), lambda i, c: (jnp.maximum(i * (tm // HALO) - 1, 0), 0)),
            _resident((1, D), lambda i, c: (0, 0)),
            _resident((1, D), lambda i, c: (0, 0)),
            pl.BlockSpec((D, fc), lambda i, c: (layer, c)),
            pl.BlockSpec((D, fc), lambda i, c: (layer, c + n_f)),
            pl.BlockSpec((CONV_W, fc), lambda i, c: (0, c)),
            pl.BlockSpec((CONV_W, fc), lambda i, c: (0, c + n_f)),
            pl.BlockSpec((1, fc), lambda i, c: (0, c)),
            pl.BlockSpec((1, fc), lambda i, c: (0, c + n_f)),
            pl.BlockSpec((fc, D), lambda i, c: (layer * n_f + c, 0)),
        ],
        out_specs=pl.BlockSpec((tm, D), lambda i, c: (i, 0)),
        out_shape=jax.ShapeDtypeStruct((T, D), _F32),
        scratch_shapes=[
            pltpu.VMEM((tm + HALO, D), _BF),
            pltpu.VMEM((tm // rb, rb + HALO, fc), _F32),
            pltpu.VMEM((tm // rb, rb + HALO, fc), _F32),
        ],
        compiler_params=_params(2),
        name="conv_ffn",
    )(r, r, g_pre, g_post, w_up, w_up, conv_w, conv_w, conv_b, conv_b, w_down)


def _ple_kernel(r_ref, p_ref, win_ref, wgate_ref, ge_ref, gr_ref, out_ref):
    r = r_ref[...]
    e = _rms(_dot(p_ref[...].astype(_BF), win_ref[...]), ge_ref[...])
    gate = _sigmoid(_dot(_rms(r, gr_ref[...]).astype(_BF), wgate_ref[...]))
    out_ref[...] = r + gate * e


def _ple_gate(r, p, w_in, w_gate, g_e, g_r):
    T, D = r.shape
    P = p.shape[1]
    tm = min(1024, T)
    return pl.pallas_call(
        _ple_kernel,
        grid=(T // tm,),
        in_specs=[
            pl.BlockSpec((tm, D), lambda i: (i, 0)),
            pl.BlockSpec((tm, P), lambda i: (i, 0)),
            _resident((P, D), lambda i: (0, 0)),
            _resident((D, D), lambda i: (0, 0)),
            _resident((1, D), lambda i: (0, 0)),
            _resident((1, D), lambda i: (0, 0)),
        ],
        out_specs=pl.BlockSpec((tm, D), lambda i: (i, 0)),
        out_shape=jax.ShapeDtypeStruct((T, D), _F32),
        compiler_params=_params(1),
        name="ple_gate",
    )(r, p, w_in, w_gate, g_e, g_r)


def kernel(x, p, norm_g, hgrn_w_in, hgrn_lb_logits, hgrn_norm_g, hgrn_w_out, gmlp_w_in, gmlp_ln_g,
           gmlp_ln_b, gmlp_w_s, gmlp_b_s, gmlp_w_out, ffn_w_up, ffn_conv_w, ffn_conv_b, ffn_w_down,
           ple_w_in, ple_w_gate, ple_norm_g):
    B, S, D = x.shape
    depth = p.shape[0]
    H = D // LANES
    T = B * S
    row = lambda a: a.reshape(1, -1)

    lbl = hgrn_lb_logits.astype(_F32).reshape(-1, H, LANES).transpose(1, 0, 2)

    F = ffn_w_down.shape[1]
    ffn_up = ffn_w_up.reshape(depth * D, 2 * F)
    ffn_down = ffn_w_down.reshape(depth * F, D)
    ffn_up_bf = ffn_down_bf = None

    r = x.reshape(T, D)
    for i in range(depth):
        j = i // 2
        if i % 2 == 0:
            to_cast = (ffn_up, ffn_down) if ffn_up_bf is None else ()
            y, cast = _hgrn_mixer(r, row(norm_g[i, 0]), _hgrn_weights(hgrn_w_in[j]), lbl, row(hgrn_norm_g[j]), to_cast,
                                  layer=i, batch=B, seq=S)
            if to_cast:
                ffn_up_bf, ffn_down_bf = cast
            w_out = hgrn_w_out[j]
        else:
            y = _gmlp_mixer(r, row(norm_g[i, 0]), gmlp_w_in[j].astype(_BF), row(gmlp_ln_g[j]),
                            row(gmlp_ln_b[j]), gmlp_w_s[j], gmlp_b_s[j].T)
            w_out = gmlp_w_out[j]
        r = _proj_residual(y, w_out.astype(_BF), r, row(norm_g[i, 1]))
        r = _conv_ffn(r, row(norm_g[i, 2]), row(norm_g[i, 3]), ffn_up_bf, ffn_conv_w[i],
                      row(ffn_conv_b[i]), ffn_down_bf, layer=i, seq=S)
        r = _ple_gate(r, p[i].reshape(T, -1), ple_w_in[i].astype(_BF), ple_w_gate[i].astype(_BF),
                      row(ple_norm_g[i, 0]), row(ple_norm_g[i, 1]))
    return r.reshape(B, S, D)
```

```python
import functools

import jax
import jax.numpy as jnp
from jax import lax
from jax.experimental import pallas as pl
from jax.experimental.pallas import tpu as pltpu

EPS = 1e-6
LANES = 128
HGRN_CHUNK = 64
HGRN_SAFE_DECAY = 60.0
HGRN_HEADS_PER_STEP = 2
GMLP_CHUNK = 128
CONV_W = 3
HALO = 16
NORM_ROWS = 128
VMEM_LIMIT = 60 * 1024 * 1024

_BF = jnp.bfloat16
_F32 = jnp.float32


def _params(n_axes):
    return pltpu.CompilerParams(
        dimension_semantics=("arbitrary",) * n_axes,
        vmem_limit_bytes=VMEM_LIMIT,
    )


def _resident(shape, index_map):
    return pl.BlockSpec(shape, index_map, pipeline_mode=pl.Buffered(1))


def _rms(x, g):
    return x * lax.rsqrt(jnp.mean(x * x, axis=-1, keepdims=True) + EPS) * g


def _sigmoid(x):
    return 1.0 / (1.0 + jnp.exp(-x))


def _dot(a, b):
    return jnp.dot(a, b, preferred_element_type=_F32)


def _dot_nt(a, b):
    return lax.dot_general(a, b, (((1,), (1,)), ((), ())), preferred_element_type=_F32)


def _dot_tn(a, b):
    return lax.dot_general(a, b, (((0,), (0,)), ((), ())), preferred_element_type=_F32)


def _chunk_cumsum(x, pos):
    shift = 1
    while shift < HGRN_CHUNK:
        x = x + jnp.where(pos >= shift, pltpu.roll(x, shift, axis=0), 0.0)
        shift *= 2
    return x


def _hgrn_gates(proj, lb):
    q = proj[:, 0:LANES]
    f = lb + (1.0 - lb) * _sigmoid(proj[:, LANES:2 * LANES])
    v = proj[:, 2 * LANES:3 * LANES]
    og = proj[:, 3 * LANES:4 * LANES]
    pos = lax.broadcasted_iota(jnp.int32, f.shape, 0) % HGRN_CHUNK
    return q, 1.0 - f, v, og, _chunk_cumsum(jnp.log(f), pos)


def _hgrn_out(o, og, ng):
    return (_rms(o, ng) * (og * _sigmoid(og))).astype(_BF)


def _hgrn_wcast_kernel(*refs):
    out_ref = refs[-1]
    for s, w_ref in enumerate(refs[:-1]):
        out_ref[:, s * LANES:(s + 1) * LANES] = w_ref[...].astype(_BF)


def _hgrn_weights(w_in):
    D = w_in.shape[0]
    H = D // LANES
    hp = HGRN_HEADS_PER_STEP
    specs = [pl.BlockSpec((D, LANES), lambda p, hh=hh, g=g: (0, g * H + p * hp + hh))
             for hh in range(hp) for g in range(4)]
    return pl.pallas_call(
        _hgrn_wcast_kernel,
        grid=(H // hp,),
        in_specs=specs,
        out_specs=pl.BlockSpec((None, D, hp * 4 * LANES), lambda p: (p, 0, 0)),
        out_shape=jax.ShapeDtypeStruct((H // hp, D, hp * 4 * LANES), _BF),
        compiler_params=_params(1),
        name="hgrn_weights",
    )(*([w_in] * (4 * hp)))


def _hgrn_kernel(r_ref, gpre_ref, win_ref, lbl_ref, ng_ref, *rest, layer, tm, rb, n_cast):
    cast_in, rest = rest[:n_cast], rest[n_cast:]
    y_ref, cast_out = rest[0], rest[1:1 + n_cast]
    hn_ref, st_ref, proj_ref, o_ref, fq_ref, fb_ref = rest[1 + n_cast:]
    n = pl.program_id(1)
    p = pl.program_id(2)
    heads = range(HGRN_HEADS_PER_STEP)
    head_cols = lambda hh, w: slice(hh * w, (hh + 1) * w)

    for src_ref, dst_ref in zip(cast_in, cast_out):
        dst_ref[...] = src_ref[...].astype(_BF)

    @pl.when(p == 0)
    def _():
        def norm_rows(j, carry):
            rows = pl.ds(pl.multiple_of(j * NORM_ROWS, NORM_ROWS), NORM_ROWS)
            hn_ref[rows, :] = _rms(r_ref[rows, :], gpre_ref[...]).astype(_BF)
            return carry

        lax.fori_loop(0, tm // NORM_ROWS, norm_rows, 0)

    @pl.when(n == 0)
    def _():
        for hh in heads:
            st_ref[p * HGRN_HEADS_PER_STEP + hh] = jnp.zeros((LANES, LANES), _F32)

    lbs = []
    for hh in heads:
        lg = lbl_ref[hh]
        ex = jnp.exp(lg - jnp.max(lg, axis=0, keepdims=True))
        lbs.append(jnp.sum(ex[0:layer + 1], axis=0, keepdims=True) / jnp.sum(ex, axis=0, keepdims=True))

    n_blocks = tm // rb
    for qb in range(n_blocks):
        rows = pl.ds(qb * rb, rb)
        proj_ref[rows, :] = _dot(hn_ref[rows, :], win_ref[...])

    row = lax.broadcasted_iota(jnp.int32, (HGRN_CHUNK, HGRN_CHUNK), 0)
    col = lax.broadcasted_iota(jnp.int32, (HGRN_CHUNK, HGRN_CHUNK), 1)
    causal = row >= col

    sts = [st_ref[p * HGRN_HEADS_PER_STEP + hh] for hh in heads]
    any_unsafe = jnp.zeros((1, LANES), _F32)
    for qb in range(n_blocks):
        rows = pl.ds(qb * rb, rb)
        gates = [_hgrn_gates(proj_ref[rows, head_cols(hh, 4 * LANES)], lbs[hh]) for hh in heads]
        pairs = [(c, hh) for c in range(rb // HGRN_CHUNK) for hh in heads]
        q_mid, k_mid, q_dec, k_dec, vb, d_last = {}, {}, {}, {}, {}, {}
        for c, hh in pairs:
            sl = slice(c * HGRN_CHUNK, (c + 1) * HGRN_CHUNK)
            q, k, v, _, bcum = gates[hh]
            b = bcum[sl]
            qc, kc = q[sl], k[sl]
            b_mid = b[HGRN_CHUNK // 2 - 1:HGRN_CHUNK // 2]
            b_last = b[HGRN_CHUNK - 1:HGRN_CHUNK]
            unsafe = b_last < -HGRN_SAFE_DECAY
            any_unsafe = jnp.maximum(any_unsafe, unsafe.astype(_F32))
            q_mid[c, hh] = jnp.where(unsafe, 0.0, qc * jnp.exp(b - b_mid)).astype(_BF)
            k_mid[c, hh] = jnp.where(unsafe, 0.0, kc * jnp.exp(b_mid - b)).astype(_BF)
            q_dec[c, hh] = (qc * jnp.exp(b)).astype(_BF)
            k_dec[c, hh] = (kc * jnp.exp(b_last - b)).astype(_BF)
            vb[c, hh] = v[sl].astype(_BF)
            d_last[c, hh] = jnp.exp(b_last)
        scores = {ph: _dot_nt(q_mid[ph], k_mid[ph]) for ph in pairs}
        st_add = {ph: _dot_tn(vb[ph], k_dec[ph]) for ph in pairs}
        o_intra = {ph: _dot(jnp.where(causal, scores[ph], 0.0).astype(_BF), vb[ph]) for ph in pairs}
        st_in = {}
        for c, hh in pairs:
            st_in[c, hh] = sts[hh]
            sts[hh] = sts[hh] * d_last[c, hh] + st_add[c, hh]
        o_inter = {ph: _dot_nt(q_dec[ph], st_in[ph].astype(_BF)) for ph in pairs}
        for hh in heads:
            o = jnp.concatenate([o_intra[c, hh] + o_inter[c, hh] for c in range(rb // HGRN_CHUNK)], axis=0)
            o_ref[hh, rows, :] = o
            y_ref[rows, head_cols(hh, LANES)] = _hgrn_out(o, gates[hh][3], ng_ref[...])
    for hh in heads:
        st_ref[p * HGRN_HEADS_PER_STEP + hh] = sts[hh]

    @pl.when(jnp.max(any_unsafe) > 0.0)
    def _():
        s_idx = lax.broadcasted_iota(jnp.int32, (HGRN_CHUNK, 1), 0)

        for hh in heads:
            def chunk_body(c, carry, hh=hh):
                base = pl.multiple_of(c * HGRN_CHUNK, HGRN_CHUNK)
                chunk = pl.ds(base, HGRN_CHUNK)
                q, k, v, og, b = _hgrn_gates(proj_ref[chunk, head_cols(hh, 4 * LANES)], lbs[hh])
                unsafe = b[HGRN_CHUNK - 1:HGRN_CHUNK] < -HGRN_SAFE_DECAY
                fq_ref[...] = jnp.where(unsafe, q, 0.0)
                fb_ref[...] = b

                def row_body(t, carry2):
                    qt = fq_ref[pl.ds(t, 1), :]
                    bt = fb_ref[pl.ds(t, 1), :]
                    w = qt * k * jnp.exp(jnp.minimum(bt - b, 0.0))
                    w = jnp.where(s_idx <= t, w, 0.0)
                    sc = jnp.sum(w, axis=1, keepdims=True)
                    o_ref[hh, pl.ds(base + t, 1), :] += jnp.sum(sc * v, axis=0, keepdims=True)
                    return carry2

                lax.fori_loop(0, HGRN_CHUNK, row_body, 0)
                y_ref[chunk, head_cols(hh, LANES)] = _hgrn_out(o_ref[hh, chunk, :], og, ng_ref[...])
                return carry

            lax.fori_loop(0, tm // HGRN_CHUNK, chunk_body, 0)


def _hgrn_mixer(r, g_pre, w_in_groups, lb_logits_heads, norm_g, to_cast, *, layer, batch, seq):
    T, D = r.shape
    H = D // LANES
    hp = HGRN_HEADS_PER_STEP
    tm = min(1024, seq)
    rb = min(512, tm)
    n_s = seq // tm
    n_steps = batch * n_s * (H // hp)
    step = lambda b, n, p: (b * n_s + n) * (H // hp) + p
    cast_specs = []
    for a in to_cast:
        rows = a.shape[0] // n_steps
        assert rows * n_steps == a.shape[0] and rows % 16 == 0, a.shape
        cast_specs.append(pl.BlockSpec((rows, a.shape[1]), lambda b, n, p: (step(b, n, p), 0)))
    kern = functools.partial(_hgrn_kernel, layer=layer, tm=tm, rb=rb, n_cast=len(to_cast))
    out = pl.pallas_call(
        kern,
        grid=(batch, n_s, H // hp),
        in_specs=[
            pl.BlockSpec((tm, D), lambda b, n, p: (b * n_s + n, 0)),
            _resident((1, D), lambda b, n, p: (0, 0)),
            pl.BlockSpec((None, D, hp * 4 * LANES), lambda b, n, p: (p, 0, 0)),
            pl.BlockSpec((hp,) + lb_logits_heads.shape[1:], lambda b, n, p: (p, 0, 0)),
            _resident((1, LANES), lambda b, n, p: (0, 0)),
        ] + cast_specs,
        out_specs=[pl.BlockSpec((tm, hp * LANES), lambda b, n, p: (b * n_s + n, p))] + cast_specs,
        out_shape=[jax.ShapeDtypeStruct((T, D), _BF)]
        + [jax.ShapeDtypeStruct(a.shape, _BF) for a in to_cast],
        scratch_shapes=[
            pltpu.VMEM((tm, D), _BF),
            pltpu.VMEM((H, LANES, LANES), _F32),
            pltpu.VMEM((tm, hp * 4 * LANES), _F32),
            pltpu.VMEM((hp, tm, LANES), _F32),
            pltpu.VMEM((HGRN_CHUNK, LANES), _F32),
            pltpu.VMEM((HGRN_CHUNK, LANES), _F32),
        ],
        compiler_params=_params(3),
        name="hgrn_mixer",
    )(r, g_pre, w_in_groups, lb_logits_heads, norm_g, *to_cast)
    return out[0], out[1:]


def _gelu_erf(x):
    return 0.5 * x * (1.0 + lax.erf(x * (2.0 ** -0.5)))


def _gmlp_kernel(r_ref, gpre_ref, win_ref, lng_ref, lnb_ref, ws_ref, bst_ref, y_ref, *, rb):
    tm, D = r_ref.shape
    row = lax.broadcasted_iota(jnp.int32, (GMLP_CHUNK, GMLP_CHUNK), 0)
    col = lax.broadcasted_iota(jnp.int32, (GMLP_CHUNK, GMLP_CHUNK), 1)
    causal = row >= col
    groups = range(D // LANES)
    w_s = [jnp.where(causal, ws_ref[g], 0.0).astype(_BF) for g in groups]

    for q in range(tm // rb):
        hn = _rms(r_ref[pl.ds(q * rb, rb), :], gpre_ref[...]).astype(_BF)
        u = _gelu_erf(_dot(hn, win_ref[:, 0:D]))
        v = _gelu_erf(_dot(hn, win_ref[:, D:2 * D]))
        mu = jnp.mean(v, axis=-1, keepdims=True)
        vc = v - mu
        var = jnp.mean(vc * vc, axis=-1, keepdims=True)
        vb = (vc * lax.rsqrt(var + EPS) * lng_ref[...] + lnb_ref[...]).astype(_BF)
        for g in groups:
            cs = slice(g * LANES, (g + 1) * LANES)
            bias = bst_ref[:, g:g + 1]
            for c in range(rb // GMLP_CHUNK):
                rs = slice(c * GMLP_CHUNK, (c + 1) * GMLP_CHUNK)
                mixed = _dot(w_s[g], vb[rs, cs]) + bias
                y_ref[pl.ds(q * rb + c * GMLP_CHUNK, GMLP_CHUNK), cs] = (u[rs, cs] * mixed).astype(_BF)


def _gmlp_mixer(r, g_pre, w_in, ln_g, ln_b, w_s, b_s_t):
    T, D = r.shape
    G = D // LANES
    tm = min(512, T)
    return pl.pallas_call(
        functools.partial(_gmlp_kernel, rb=min(512, tm)),
        grid=(T // tm,),
        in_specs=[
            pl.BlockSpec((tm, D), lambda i: (i, 0)),
            _resident((1, D), lambda i: (0, 0)),
            _resident((D, 2 * D), lambda i: (0, 0)),
            _resident((1, D), lambda i: (0, 0)),
            _resident((1, D), lambda i: (0, 0)),
            _resident((G, GMLP_CHUNK, GMLP_CHUNK), lambda i: (0, 0, 0)),
            _resident((GMLP_CHUNK, G), lambda i: (0, 0)),
        ],
        out_specs=pl.BlockSpec((tm, D), lambda i: (i, 0)),
        out_shape=jax.ShapeDtypeStruct((T, D), _BF),
        compiler_params=_params(1),
        name="gmlp_mixer",
    )(r, g_pre, w_in, ln_g, ln_b, w_s, b_s_t)


def _proj_kernel(y_ref, w_ref, r_ref, g_ref, out_ref):
    out_ref[...] = r_ref[...] + _rms(_dot(y_ref[...], w_ref[...]), g_ref[...])


def _proj_residual(y, w, r, g_post):
    T, D = r.shape
    tm = min(1024, T)
    return pl.pallas_call(
        _proj_kernel,
        grid=(T // tm,),
        in_specs=[
            pl.BlockSpec((tm, D), lambda i: (i, 0)),
            _resident((D, D), lambda i: (0, 0)),
            pl.BlockSpec((tm, D), lambda i: (i, 0)),
            _resident((1, D), lambda i: (0, 0)),
        ],
        out_specs=pl.BlockSpec((tm, D), lambda i: (i, 0)),
        out_shape=jax.ShapeDtypeStruct((T, D), _F32),
        compiler_params=_params(1),
        name="proj_residual",
    )(y, w, r, g_post)


def _gelu_tanh(x):
    return 0.5 * x * (1.0 + jnp.tanh((2.0 / jnp.pi) ** 0.5 * (x + 0.044715 * (x * x * x))))


def _ffn_kernel(r_ref, gpre_ref, gpost_ref, wg_ref, wv_ref, cwg_ref, cwv_ref, cbg_ref, cbv_ref, wd_ref,
                out_ref, hn_ref, hg_ref, hv_ref, tailg_ref, tailv_ref, *, tm, rb, n_chunks, tiles_per_seq):
    i = pl.program_id(0)
    c = pl.program_id(1)
    n_blocks = tm // rb

    def conv(h_ref, q, cw_ref, cb_ref):
        acc = cb_ref[...] + h_ref[q, pl.ds(HALO, rb), :] * cw_ref[CONV_W - 1:CONV_W, :]
        for j in range(1, CONV_W):
            acc = acc + h_ref[q, pl.ds(HALO - j, rb), :] * cw_ref[CONV_W - 1 - j:CONV_W - j, :]
        return acc

    def chunk_step(first, last):
        top = pl.ds(HALO - 8, 8)
        last_rows = pl.ds(HALO + rb - 8, 8)
        seq_start = i % tiles_per_seq == 0

        @pl.when(seq_start)
        def _():
            hg_ref[0, top, :] = jnp.zeros((8, hg_ref.shape[2]), _F32)
            hv_ref[0, top, :] = jnp.zeros((8, hv_ref.shape[2]), _F32)

        @pl.when(jnp.logical_not(seq_start))
        def _():
            hg_ref[0, top, :] = tailg_ref[c]
            hv_ref[0, top, :] = tailv_ref[c]

        for q in range(n_blocks):
            rows = pl.ds(q * rb, rb)
            if first:
                hn_ref[rows, :] = _rms(r_ref[rows, :], gpre_ref[...]).astype(_BF)
            for h_ref, w_ref in ((hg_ref, wg_ref), (hv_ref, wv_ref)):
                h_ref[q, pl.ds(HALO, rb), :] = _dot(hn_ref[rows, :], w_ref[...])
                if q > 0:
                    h_ref[q, top, :] = h_ref[q - 1, last_rows, :]
        tailg_ref[c] = hg_ref[n_blocks - 1, last_rows, :]
        tailv_ref[c] = hv_ref[n_blocks - 1, last_rows, :]

        for q in range(n_blocks):
            rows = pl.ds(q * rb, rb)
            gate = conv(hg_ref, q, cwg_ref, cbg_ref)
            val = conv(hv_ref, q, cwv_ref, cbv_ref)
            act = (_gelu_tanh(gate) * val).astype(_BF)
            down = _dot(act, wd_ref[...])
            if not first:
                down = out_ref[rows, :] + down
            if last:
                down = r_ref[rows, :] + _rms(down, gpost_ref[...])
            out_ref[rows, :] = down

    n_f = pl.num_programs(1)
    pl.when(c == 0)(lambda: chunk_step(True, n_chunks == 1))
    if n_chunks > 1:
        pl.when(c == n_f - 1)(lambda: chunk_step(False, True))
    if n_chunks > 2:
        pl.when(jnp.logical_and(c > 0, c < n_f - 1))(lambda: chunk_step(False, False))


def _conv_ffn(r, g_pre, g_post, w_up, conv_w, conv_b, w_down, *, layer, seq):
    T, D = r.shape
    F = w_up.shape[1] // 2
    tm = min(1024, seq)
    rb = min(512, tm)
    fc = min(512, F)
    n_f = F // fc
    kern = functools.partial(_ffn_kernel, tm=tm, rb=rb, n_chunks=n_f, tiles_per_seq=seq // tm)
    return pl.pallas_call(
        kern,
        grid=(T // tm, n_f),
        in_specs=[
            pl.BlockSpec((tm, D), lambda i, c: (i, 0)),
            _resident((1, D), lambda i, c: (0, 0)),
            _resident((1, D), lambda i, c: (0, 0)),
            pl.BlockSpec((D, fc), lambda i, c: (layer, c)),
            pl.BlockSpec((D, fc), lambda i, c: (layer, c + n_f)),
            pl.BlockSpec((CONV_W, fc), lambda i, c: (0, c)),
            pl.BlockSpec((CONV_W, fc), lambda i, c: (0, c + n_f)),
            pl.BlockSpec((1, fc), lambda i, c: (0, c)),
            pl.BlockSpec((1, fc), lambda i, c: (0, c + n_f)),
            pl.BlockSpec((fc, D), lambda i, c: (layer * n_f + c, 0)),
        ],
        out_specs=pl.BlockSpec((tm, D), lambda i, c: (i, 0)),
        out_shape=jax.ShapeDtypeStruct((T, D), _F32),
        scratch_shapes=[
            pltpu.VMEM((tm, D), _BF),
            pltpu.VMEM((tm // rb, rb + HALO, fc), _F32),
            pltpu.VMEM((tm // rb, rb + HALO, fc), _F32),
            pltpu.VMEM((n_f, 8, fc), _F32),
            pltpu.VMEM((n_f, 8, fc), _F32),
        ],
        compiler_params=_params(2),
        name="conv_ffn",
    )(r, g_pre, g_post, w_up, w_up, conv_w, conv_w, conv_b, conv_b, w_down)


def _ple_kernel(r_ref, p_ref, win_ref, wgate_ref, ge_ref, gr_ref, out_ref):
    r = r_ref[...]
    e = _rms(_dot(p_ref[...].astype(_BF), win_ref[...]), ge_ref[...])
    gate = _sigmoid(_dot(_rms(r, gr_ref[...]).astype(_BF), wgate_ref[...]))
    out_ref[...] = r + gate * e


def _ple_gate(r, p, w_in, w_gate, g_e, g_r):
    T, D = r.shape
    P = p.shape[1]
    tm = min(1024, T)
    return pl.pallas_call(
        _ple_kernel,
        grid=(T // tm,),
        in_specs=[
            pl.BlockSpec((tm, D), lambda i: (i, 0)),
            pl.BlockSpec((tm, P), lambda i: (i, 0)),
            _resident((P, D), lambda i: (0, 0)),
            _resident((D, D), lambda i: (0, 0)),
            _resident((1, D), lambda i: (0, 0)),
            _resident((1, D), lambda i: (0, 0)),
        ],
        out_specs=pl.BlockSpec((tm, D), lambda i: (i, 0)),
        out_shape=jax.ShapeDtypeStruct((T, D), _F32),
        compiler_params=_params(1),
        name="ple_gate",
    )(r, p, w_in, w_gate, g_e, g_r)


def kernel(x, p, norm_g, hgrn_w_in, hgrn_lb_logits, hgrn_norm_g, hgrn_w_out, gmlp_w_in, gmlp_ln_g,
           gmlp_ln_b, gmlp_w_s, gmlp_b_s, gmlp_w_out, ffn_w_up, ffn_conv_w, ffn_conv_b, ffn_w_down,
           ple_w_in, ple_w_gate, ple_norm_g):
    B, S, D = x.shape
    depth = p.shape[0]
    H = D // LANES
    T = B * S
    row = lambda a: a.reshape(1, -1)

    lbl = hgrn_lb_logits.astype(_F32).reshape(-1, H, LANES).transpose(1, 0, 2)

    F = ffn_w_down.shape[1]
    ffn_up = ffn_w_up.reshape(depth * D, 2 * F)
    ffn_down = ffn_w_down.reshape(depth * F, D)
    ffn_up_bf = ffn_down_bf = None

    r = x.reshape(T, D)
    for i in range(depth):
        j = i // 2
        if i % 2 == 0:
            to_cast = (ffn_up, ffn_down) if ffn_up_bf is None else ()
            y, cast = _hgrn_mixer(r, row(norm_g[i, 0]), _hgrn_weights(hgrn_w_in[j]), lbl, row(hgrn_norm_g[j]), to_cast,
                                  layer=i, batch=B, seq=S)
            if to_cast:
                ffn_up_bf, ffn_down_bf = cast
            w_out = hgrn_w_out[j]
        else:
            y = _gmlp_mixer(r, row(norm_g[i, 0]), gmlp_w_in[j].astype(_BF), row(gmlp_ln_g[j]),
                            row(gmlp_ln_b[j]), gmlp_w_s[j], gmlp_b_s[j].T)
            w_out = gmlp_w_out[j]
        r = _proj_residual(y, w_out.astype(_BF), r, row(norm_g[i, 1]))
        r = _conv_ffn(r, row(norm_g[i, 2]), row(norm_g[i, 3]), ffn_up_bf, ffn_conv_w[i],
                      row(ffn_conv_b[i]), ffn_down_bf, layer=i, seq=S)
        r = _ple_gate(r, p[i].reshape(T, -1), ple_w_in[i].astype(_BF), ple_w_gate[i].astype(_BF),
                      row(ple_norm_g[i, 0]), row(ple_norm_g[i, 1]))
    return r.reshape(B, S, D)
```

```python
import functools

import jax
import jax.numpy as jnp
from jax import lax
from jax.experimental import pallas as pl
from jax.experimental.pallas import tpu as pltpu

EPS = 1e-6
LANES = 128
HGRN_CHUNK = 64
HGRN_SAFE_DECAY = 60.0
HGRN_HEADS_PER_STEP = 2
GMLP_CHUNK = 128
CONV_W = 3
HALO = 16
NORM_ROWS = 128
VMEM_LIMIT = 60 * 1024 * 1024

_BF = jnp.bfloat16
_F32 = jnp.float32


def _params(n_axes):
    return pltpu.CompilerParams(
        dimension_semantics=("arbitrary",) * n_axes,
        vmem_limit_bytes=VMEM_LIMIT,
    )


def _resident(shape, index_map):
    return pl.BlockSpec(shape, index_map, pipeline_mode=pl.Buffered(1))


def _rms(x, g):
    return x * lax.rsqrt(jnp.mean(x * x, axis=-1, keepdims=True) + EPS) * g


def _sigmoid(x):
    return 1.0 / (1.0 + jnp.exp(-x))


def _dot(a, b):
    return jnp.dot(a, b, preferred_element_type=_F32)


def _dot_nt(a, b):
    return lax.dot_general(a, b, (((1,), (1,)), ((), ())), preferred_element_type=_F32)


def _dot_tn(a, b):
    return lax.dot_general(a, b, (((0,), (0,)), ((), ())), preferred_element_type=_F32)


def _chunk_cumsum(x, pos):
    shift = 1
    while shift < HGRN_CHUNK:
        x = x + jnp.where(pos >= shift, pltpu.roll(x, shift, axis=0), 0.0)
        shift *= 2
    return x


def _hgrn_gates(proj, lb):
    q = proj[:, 0:LANES]
    f = lb + (1.0 - lb) * _sigmoid(proj[:, LANES:2 * LANES])
    v = proj[:, 2 * LANES:3 * LANES]
    og = proj[:, 3 * LANES:4 * LANES]
    pos = lax.broadcasted_iota(jnp.int32, f.shape, 0) % HGRN_CHUNK
    return q, 1.0 - f, v, og, _chunk_cumsum(jnp.log(f), pos)


def _hgrn_out(o, og, ng):
    return (_rms(o, ng) * (og * _sigmoid(og))).astype(_BF)


def _hgrn_wcast_kernel(*refs):
    out_ref = refs[-1]
    for s, w_ref in enumerate(refs[:-1]):
        out_ref[:, s * LANES:(s + 1) * LANES] = w_ref[...].astype(_BF)


def _hgrn_weights(w_in):
    D = w_in.shape[0]
    H = D // LANES
    hp = HGRN_HEADS_PER_STEP
    specs = [pl.BlockSpec((D, LANES), lambda p, hh=hh, g=g: (0, g * H + p * hp + hh))
             for hh in range(hp) for g in range(4)]
    return pl.pallas_call(
        _hgrn_wcast_kernel,
        grid=(H // hp,),
        in_specs=specs,
        out_specs=pl.BlockSpec((None, D, hp * 4 * LANES), lambda p: (p, 0, 0)),
        out_shape=jax.ShapeDtypeStruct((H // hp, D, hp * 4 * LANES), _BF),
        compiler_params=_params(1),
        name="hgrn_weights",
    )(*([w_in] * (4 * hp)))


def _hgrn_kernel(r_ref, gpre_ref, win_ref, lbl_ref, ng_ref, *rest, layer, tm, rb, n_cast):
    cast_in, rest = rest[:n_cast], rest[n_cast:]
    y_ref, cast_out = rest[0], rest[1:1 + n_cast]
    hn_ref, st_ref, proj_ref, o_ref, fq_ref, fb_ref = rest[1 + n_cast:]
    n = pl.program_id(1)
    p = pl.program_id(2)
    heads = range(HGRN_HEADS_PER_STEP)
    head_cols = lambda hh, w: slice(hh * w, (hh + 1) * w)

    for src_ref, dst_ref in zip(cast_in, cast_out):
        dst_ref[...] = src_ref[...].astype(_BF)

    @pl.when(p == 0)
    def _():
        def norm_rows(j, carry):
            rows = pl.ds(pl.multiple_of(j * NORM_ROWS, NORM_ROWS), NORM_ROWS)
            hn_ref[rows, :] = _rms(r_ref[rows, :], gpre_ref[...]).astype(_BF)
            return carry

        lax.fori_loop(0, tm // NORM_ROWS, norm_rows, 0)

    @pl.when(n == 0)
    def _():
        for hh in heads:
            st_ref[p * HGRN_HEADS_PER_STEP + hh] = jnp.zeros((LANES, LANES), _F32)

    lbs = []
    for hh in heads:
        lg = lbl_ref[hh]
        ex = jnp.exp(lg - jnp.max(lg, axis=0, keepdims=True))
        lbs.append(jnp.sum(ex[0:layer + 1], axis=0, keepdims=True) / jnp.sum(ex, axis=0, keepdims=True))

    n_blocks = tm // rb
    for qb in range(n_blocks):
        rows = pl.ds(qb * rb, rb)
        proj_ref[rows, :] = _dot(hn_ref[rows, :], win_ref[...])

    row = lax.broadcasted_iota(jnp.int32, (HGRN_CHUNK, HGRN_CHUNK), 0)
    col = lax.broadcasted_iota(jnp.int32, (HGRN_CHUNK, HGRN_CHUNK), 1)
    causal = row >= col

    sts = [st_ref[p * HGRN_HEADS_PER_STEP + hh] for hh in heads]
    any_unsafe = jnp.zeros((1, LANES), _F32)
    for qb in range(n_blocks):
        rows = pl.ds(qb * rb, rb)
        gates = [_hgrn_gates(proj_ref[rows, head_cols(hh, 4 * LANES)], lbs[hh]) for hh in heads]
        pairs = [(c, hh) for c in range(rb // HGRN_CHUNK) for hh in heads]
        q_mid, k_mid, q_dec, k_dec, vb, d_last = {}, {}, {}, {}, {}, {}
        for c, hh in pairs:
            sl = slice(c * HGRN_CHUNK, (c + 1) * HGRN_CHUNK)
            q, k, v, _, bcum = gates[hh]
            b = bcum[sl]
            qc, kc = q[sl], k[sl]
            b_mid = b[HGRN_CHUNK // 2 - 1:HGRN_CHUNK // 2]
            b_last = b[HGRN_CHUNK - 1:HGRN_CHUNK]
            unsafe = b_last < -HGRN_SAFE_DECAY
            any_unsafe = jnp.maximum(any_unsafe, unsafe.astype(_F32))
            q_mid[c, hh] = jnp.where(unsafe, 0.0, qc * jnp.exp(b - b_mid)).astype(_BF)
            k_mid[c, hh] = jnp.where(unsafe, 0.0, kc * jnp.exp(b_mid - b)).astype(_BF)
            q_dec[c, hh] = (qc * jnp.exp(b)).astype(_BF)
            k_dec[c, hh] = (kc * jnp.exp(b_last - b)).astype(_BF)
            vb[c, hh] = v[sl].astype(_BF)
            d_last[c, hh] = jnp.exp(b_last)
        scores = {ph: _dot_nt(q_mid[ph], k_mid[ph]) for ph in pairs}
        st_add = {ph: _dot_tn(vb[ph], k_dec[ph]) for ph in pairs}
        o_intra = {ph: _dot(jnp.where(causal, scores[ph], 0.0).astype(_BF), vb[ph]) for ph in pairs}
        st_in = {}
        for c, hh in pairs:
            st_in[c, hh] = sts[hh]
            sts[hh] = sts[hh] * d_last[c, hh] + st_add[c, hh]
        o_inter = {ph: _dot_nt(q_dec[ph], st_in[ph].astype(_BF)) for ph in pairs}
        for hh in heads:
            o = jnp.concatenate([o_intra[c, hh] + o_inter[c, hh] for c in range(rb // HGRN_CHUNK)], axis=0)
            o_ref[hh, rows, :] = o
            y_ref[rows, head_cols(hh, LANES)] = _hgrn_out(o, gates[hh][3], ng_ref[...])
    for hh in heads:
        st_ref[p * HGRN_HEADS_PER_STEP + hh] = sts[hh]

    @pl.when(jnp.max(any_unsafe) > 0.0)
    def _():
        s_idx = lax.broadcasted_iota(jnp.int32, (HGRN_CHUNK, 1), 0)

        for hh in heads:
            def chunk_body(c, carry, hh=hh):
                base = pl.multiple_of(c * HGRN_CHUNK, HGRN_CHUNK)
                chunk = pl.ds(base, HGRN_CHUNK)
                q, k, v, og, b = _hgrn_gates(proj_ref[chunk, head_cols(hh, 4 * LANES)], lbs[hh])
                unsafe = b[HGRN_CHUNK - 1:HGRN_CHUNK] < -HGRN_SAFE_DECAY
                fq_ref[...] = jnp.where(unsafe, q, 0.0)
                fb_ref[...] = b

                def row_body(t, carry2):
                    qt = fq_ref[pl.ds(t, 1), :]
                    bt = fb_ref[pl.ds(t, 1), :]
                    w = qt * k * jnp.exp(jnp.minimum(bt - b, 0.0))
                    w = jnp.where(s_idx <= t, w, 0.0)
                    sc = jnp.sum(w, axis=1, keepdims=True)
                    o_ref[hh, pl.ds(base + t, 1), :] += jnp.sum(sc * v, axis=0, keepdims=True)
                    return carry2

                lax.fori_loop(0, HGRN_CHUNK, row_body, 0)
                y_ref[chunk, head_cols(hh, LANES)] = _hgrn_out(o_ref[hh, chunk, :], og, ng_ref[...])
                return carry

            lax.fori_loop(0, tm // HGRN_CHUNK, chunk_body, 0)


def _hgrn_mixer(r, g_pre, w_in_groups, lb_logits_heads, norm_g, to_cast, *, layer, batch, seq):
    T, D = r.shape
    H = D // LANES
    hp = HGRN_HEADS_PER_STEP
    tm = min(1024, seq)
    rb = min(512, tm)
    n_s = seq // tm
    n_steps = batch * n_s * (H // hp)
    step = lambda b, n, p: (b * n_s + n) * (H // hp) + p
    cast_specs = []
    for a in to_cast:
        rows = a.shape[0] // n_steps
        assert rows * n_steps == a.shape[0] and rows % 16 == 0, a.shape
        cast_specs.append(pl.BlockSpec((rows, a.shape[1]), lambda b, n, p: (step(b, n, p), 0)))
    kern = functools.partial(_hgrn_kernel, layer=layer, tm=tm, rb=rb, n_cast=len(to_cast))
    out = pl.pallas_call(
        kern,
        grid=(batch, n_s, H // hp),
        in_specs=[
            pl.BlockSpec((tm, D), lambda b, n, p: (b * n_s + n, 0)),
            _resident((1, D), lambda b, n, p: (0, 0)),
            pl.BlockSpec((None, D, hp * 4 * LANES), lambda b, n, p: (p, 0, 0)),
            pl.BlockSpec((hp,) + lb_logits_heads.shape[1:], lambda b, n, p: (p, 0, 0)),
            _resident((1, LANES), lambda b, n, p: (0, 0)),
        ] + cast_specs,
        out_specs=[pl.BlockSpec((tm, hp * LANES), lambda b, n, p: (b * n_s + n, p))] + cast_specs,
        out_shape=[jax.ShapeDtypeStruct((T, D), _BF)]
        + [jax.ShapeDtypeStruct(a.shape, _BF) for a in to_cast],
        scratch_shapes=[
            pltpu.VMEM((tm, D), _BF),
            pltpu.VMEM((H, LANES, LANES), _F32),
            pltpu.VMEM((tm, hp * 4 * LANES), _F32),
            pltpu.VMEM((hp, tm, LANES), _F32),
            pltpu.VMEM((HGRN_CHUNK, LANES), _F32),
            pltpu.VMEM((HGRN_CHUNK, LANES), _F32),
        ],
        compiler_params=_params(3),
        name="hgrn_mixer",
    )(r, g_pre, w_in_groups, lb_logits_heads, norm_g, *to_cast)
    return out[0], out[1:]


def _gelu_erf(x):
    return 0.5 * x * (1.0 + lax.erf(x * (2.0 ** -0.5)))


def _gmlp_kernel(r_ref, gpre_ref, win_ref, lng_ref, lnb_ref, ws_ref, bst_ref, y_ref, *, rb):
    tm, D = r_ref.shape
    row = lax.broadcasted_iota(jnp.int32, (GMLP_CHUNK, GMLP_CHUNK), 0)
    col = lax.broadcasted_iota(jnp.int32, (GMLP_CHUNK, GMLP_CHUNK), 1)
    causal = row >= col
    groups = range(D // LANES)
    w_s = [jnp.where(causal, ws_ref[g], 0.0).astype(_BF) for g in groups]

    for q in range(tm // rb):
        hn = _rms(r_ref[pl.ds(q * rb, rb), :], gpre_ref[...]).astype(_BF)
        u = _gelu_erf(_dot(hn, win_ref[:, 0:D]))
        v = _gelu_erf(_dot(hn, win_ref[:, D:2 * D]))
        mu = jnp.mean(v, axis=-1, keepdims=True)
        vc = v - mu
        var = jnp.mean(vc * vc, axis=-1, keepdims=True)
        vb = (vc * lax.rsqrt(var + EPS) * lng_ref[...] + lnb_ref[...]).astype(_BF)
        for g in groups:
            cs = slice(g * LANES, (g + 1) * LANES)
            bias = bst_ref[:, g:g + 1]
            for c in range(rb // GMLP_CHUNK):
                rs = slice(c * GMLP_CHUNK, (c + 1) * GMLP_CHUNK)
                mixed = _dot(w_s[g], vb[rs, cs]) + bias
                y_ref[pl.ds(q * rb + c * GMLP_CHUNK, GMLP_CHUNK), cs] = (u[rs, cs] * mixed).astype(_BF)


def _gmlp_mixer(r, g_pre, w_in, ln_g, ln_b, w_s, b_s_t, *, index):
    T, D = r.shape
    G = D // LANES
    tm = min(512, T)
    return pl.pallas_call(
        functools.partial(_gmlp_kernel, rb=min(512, tm)),
        grid=(T // tm,),
        in_specs=[
            pl.BlockSpec((tm, D), lambda i: (i, 0)),
            _resident((1, D), lambda i: (0, 0)),
            _resident((D, 2 * D), lambda i: (index, 0)),
            _resident((1, D), lambda i: (0, 0)),
            _resident((1, D), lambda i: (0, 0)),
            _resident((G, GMLP_CHUNK, GMLP_CHUNK), lambda i: (0, 0, 0)),
            _resident((GMLP_CHUNK, G), lambda i: (0, 0)),
        ],
        out_specs=pl.BlockSpec((tm, D), lambda i: (i, 0)),
        out_shape=jax.ShapeDtypeStruct((T, D), _BF),
        compiler_params=_params(1),
        name="gmlp_mixer",
    )(r, g_pre, w_in, ln_g, ln_b, w_s, b_s_t)


def _proj_kernel(y_ref, w_ref, r_ref, g_ref, out_ref):
    out_ref[...] = r_ref[...] + _rms(_dot(y_ref[...], w_ref[...]), g_ref[...])


def _proj_residual(y, w, r, g_post, *, index):
    T, D = r.shape
    tm = min(1024, T)
    return pl.pallas_call(
        _proj_kernel,
        grid=(T // tm,),
        in_specs=[
            pl.BlockSpec((tm, D), lambda i: (i, 0)),
            _resident((D, D), lambda i: (index, 0)),
            pl.BlockSpec((tm, D), lambda i: (i, 0)),
            _resident((1, D), lambda i: (0, 0)),
        ],
        out_specs=pl.BlockSpec((tm, D), lambda i: (i, 0)),
        out_shape=jax.ShapeDtypeStruct((T, D), _F32),
        compiler_params=_params(1),
        name="proj_residual",
    )(y, w, r, g_post)


def _gelu_tanh(x):
    return 0.5 * x * (1.0 + jnp.tanh((2.0 / jnp.pi) ** 0.5 * (x + 0.044715 * (x * x * x))))


def _ffn_kernel(r_ref, gpre_ref, gpost_ref, wg_ref, wv_ref, cwg_ref, cwv_ref, cbg_ref, cbv_ref, wd_ref,
                out_ref, hn_ref, hg_ref, hv_ref, tailg_ref, tailv_ref, *, tm, rb, n_chunks, tiles_per_seq):
    i = pl.program_id(0)
    c = pl.program_id(1)
    n_blocks = tm // rb

    def conv(h_ref, q, cw_ref, cb_ref):
        acc = cb_ref[...] + h_ref[q, pl.ds(HALO, rb), :] * cw_ref[CONV_W - 1:CONV_W, :]
        for j in range(1, CONV_W):
            acc = acc + h_ref[q, pl.ds(HALO - j, rb), :] * cw_ref[CONV_W - 1 - j:CONV_W - j, :]
        return acc

    def chunk_step(first, last):
        top = pl.ds(HALO - 8, 8)
        last_rows = pl.ds(HALO + rb - 8, 8)
        seq_start = i % tiles_per_seq == 0

        @pl.when(seq_start)
        def _():
            hg_ref[0, top, :] = jnp.zeros((8, hg_ref.shape[2]), _F32)
            hv_ref[0, top, :] = jnp.zeros((8, hv_ref.shape[2]), _F32)

        @pl.when(jnp.logical_not(seq_start))
        def _():
            hg_ref[0, top, :] = tailg_ref[c]
            hv_ref[0, top, :] = tailv_ref[c]

        for q in range(n_blocks):
            rows = pl.ds(q * rb, rb)
            if first:
                hn_ref[rows, :] = _rms(r_ref[rows, :], gpre_ref[...]).astype(_BF)
            for h_ref, w_ref in ((hg_ref, wg_ref), (hv_ref, wv_ref)):
                h_ref[q, pl.ds(HALO, rb), :] = _dot(hn_ref[rows, :], w_ref[...])
                if q > 0:
                    h_ref[q, top, :] = h_ref[q - 1, last_rows, :]
        tailg_ref[c] = hg_ref[n_blocks - 1, last_rows, :]
        tailv_ref[c] = hv_ref[n_blocks - 1, last_rows, :]

        for q in range(n_blocks):
            rows = pl.ds(q * rb, rb)
            gate = conv(hg_ref, q, cwg_ref, cbg_ref)
            val = conv(hv_ref, q, cwv_ref, cbv_ref)
            act = (_gelu_tanh(gate) * val).astype(_BF)
            down = _dot(act, wd_ref[...])
            if not first:
                down = out_ref[rows, :] + down
            if last:
                down = r_ref[rows, :] + _rms(down, gpost_ref[...])
            out_ref[rows, :] = down

    n_f = pl.num_programs(1)
    pl.when(c == 0)(lambda: chunk_step(True, n_chunks == 1))
    if n_chunks > 1:
        pl.when(c == n_f - 1)(lambda: chunk_step(False, True))
    if n_chunks > 2:
        pl.when(jnp.logical_and(c > 0, c < n_f - 1))(lambda: chunk_step(False, False))


def _conv_ffn(r, g_pre, g_post, w_up, conv_w, conv_b, w_down, *, layer, seq):
    T, D = r.shape
    F = w_up.shape[1] // 2
    tm = min(1024, seq)
    rb = min(512, tm)
    fc = min(512, F)
    n_f = F // fc
    kern = functools.partial(_ffn_kernel, tm=tm, rb=rb, n_chunks=n_f, tiles_per_seq=seq // tm)
    return pl.pallas_call(
        kern,
        grid=(T // tm, n_f),
        in_specs=[
            pl.BlockSpec((tm, D), lambda i, c: (i, 0)),
            _resident((1, D), lambda i, c: (0, 0)),
            _resident((1, D), lambda i, c: (0, 0)),
            pl.BlockSpec((D, fc), lambda i, c: (layer, c)),
            pl.BlockSpec((D, fc), lambda i, c: (layer, c + n_f)),
            pl.BlockSpec((CONV_W, fc), lambda i, c: (0, c)),
            pl.BlockSpec((CONV_W, fc), lambda i, c: (0, c + n_f)),
            pl.BlockSpec((1, fc), lambda i, c: (0, c)),
            pl.BlockSpec((1, fc), lambda i, c: (0, c + n_f)),
            pl.BlockSpec((fc, D), lambda i, c: (layer * n_f + c, 0)),
        ],
        out_specs=pl.BlockSpec((tm, D), lambda i, c: (i, 0)),
        out_shape=jax.ShapeDtypeStruct((T, D), _F32),
        scratch_shapes=[
            pltpu.VMEM((tm, D), _BF),
            pltpu.VMEM((tm // rb, rb + HALO, fc), _F32),
            pltpu.VMEM((tm // rb, rb + HALO, fc), _F32),
            pltpu.VMEM((n_f, 8, fc), _F32),
            pltpu.VMEM((n_f, 8, fc), _F32),
        ],
        compiler_params=_params(2),
        name="conv_ffn",
    )(r, g_pre, g_post, w_up, w_up, conv_w, conv_w, conv_b, conv_b, w_down)


def _ple_kernel(r_ref, p_ref, win_ref, wgate_ref, ge_ref, gr_ref, out_ref):
    r = r_ref[...]
    e = _rms(_dot(p_ref[...].astype(_BF), win_ref[...]), ge_ref[...])
    gate = _sigmoid(_dot(_rms(r, gr_ref[...]).astype(_BF), wgate_ref[...]))
    out_ref[...] = r + gate * e


def _ple_gate(r, p, w_in, w_gate, g_e, g_r, *, layer):
    T, D = r.shape
    P = p.shape[1]
    tm = min(1024, T)
    return pl.pallas_call(
        _ple_kernel,
        grid=(T // tm,),
        in_specs=[
            pl.BlockSpec((tm, D), lambda i: (i, 0)),
            pl.BlockSpec((tm, P), lambda i: (i, 0)),
            _resident((P, D), lambda i: (0, 0)),
            _resident((D, D), lambda i: (layer, 0)),
            _resident((1, D), lambda i: (0, 0)),
            _resident((1, D), lambda i: (0, 0)),
        ],
        out_specs=pl.BlockSpec((tm, D), lambda i: (i, 0)),
        out_shape=jax.ShapeDtypeStruct((T, D), _F32),
        compiler_params=_params(1),
        name="ple_gate",
    )(r, p, w_in, w_gate, g_e, g_r)


def kernel(x, p, norm_g, hgrn_w_in, hgrn_lb_logits, hgrn_norm_g, hgrn_w_out, gmlp_w_in, gmlp_ln_g,
           gmlp_ln_b, gmlp_w_s, gmlp_b_s, gmlp_w_out, ffn_w_up, ffn_conv_w, ffn_conv_b, ffn_w_down,
           ple_w_in, ple_w_gate, ple_norm_g):
    B, S, D = x.shape
    depth = p.shape[0]
    H = D // LANES
    T = B * S
    row = lambda a: a.reshape(1, -1)

    lbl = hgrn_lb_logits.astype(_F32).reshape(-1, H, LANES).transpose(1, 0, 2)

    F = ffn_w_down.shape[1]
    stacks = (ffn_w_up.reshape(depth * D, 2 * F), ffn_w_down.reshape(depth * F, D),
              hgrn_w_out.reshape(-1, D), gmlp_w_in.reshape(-1, 2 * D), gmlp_w_out.reshape(-1, D),
              ple_w_gate.reshape(depth * D, D))
    bf = None

    r = x.reshape(T, D)
    for i in range(depth):
        j = i // 2
        if i % 2 == 0:
            y, cast = _hgrn_mixer(r, row(norm_g[i, 0]), _hgrn_weights(hgrn_w_in[j]), lbl, row(hgrn_norm_g[j]),
                                  stacks if bf is None else (), layer=i, batch=B, seq=S)
            if bf is None:
                bf = dict(zip(("ffn_up", "ffn_down", "hgrn_out", "gmlp_in", "gmlp_out", "ple_gate"), cast))
            w_out = bf["hgrn_out"]
        else:
            y = _gmlp_mixer(r, row(norm_g[i, 0]), bf["gmlp_in"], row(gmlp_ln_g[j]), row(gmlp_ln_b[j]),
                            gmlp_w_s[j], gmlp_b_s[j].T, index=j)
            w_out = bf["gmlp_out"]
        r = _proj_residual(y, w_out, r, row(norm_g[i, 1]), index=j)
        r = _conv_ffn(r, row(norm_g[i, 2]), row(norm_g[i, 3]), bf["ffn_up"], ffn_conv_w[i],
                      row(ffn_conv_b[i]), bf["ffn_down"], layer=i, seq=S)
        r = _ple_gate(r, p[i].reshape(T, -1), ple_w_in[i].astype(_BF), bf["ple_gate"],
                      row(ple_norm_g[i, 0]), row(ple_norm_g[i, 1]), layer=i)
    return r.reshape(B, S, D)
```

```python
import functools

import jax
import jax.numpy as jnp
from jax import lax
from jax.experimental import pallas as pl
from jax.experimental.pallas import tpu as pltpu

EPS = 1e-6
LANES = 128
HGRN_CHUNK = 64
HGRN_SAFE_DECAY = 60.0
HGRN_HEADS_PER_STEP = 2
GMLP_CHUNK = 128
CONV_W = 3
HALO = 16
NORM_ROWS = 128
VMEM_LIMIT = 60 * 1024 * 1024

_BF = jnp.bfloat16
_F32 = jnp.float32


def _params(n_axes):
    return pltpu.CompilerParams(
        dimension_semantics=("arbitrary",) * n_axes,
        vmem_limit_bytes=VMEM_LIMIT,
    )


def _resident(shape, index_map):
    return pl.BlockSpec(shape, index_map, pipeline_mode=pl.Buffered(1))


def _rms(x, g):
    return x * lax.rsqrt(jnp.mean(x * x, axis=-1, keepdims=True) + EPS) * g


def _sigmoid(x):
    return 1.0 / (1.0 + jnp.exp(-x))


def _dot(a, b):
    return jnp.dot(a, b, preferred_element_type=_F32)


def _dot_nt(a, b):
    return lax.dot_general(a, b, (((1,), (1,)), ((), ())), preferred_element_type=_F32)


def _dot_tn(a, b):
    return lax.dot_general(a, b, (((0,), (0,)), ((), ())), preferred_element_type=_F32)


def _chunk_cumsum(x, pos):
    shift = 1
    while shift < HGRN_CHUNK:
        x = x + jnp.where(pos >= shift, pltpu.roll(x, shift, axis=0), 0.0)
        shift *= 2
    return x


def _hgrn_gates(proj, lb):
    q = proj[:, 0:LANES]
    f = lb + (1.0 - lb) * _sigmoid(proj[:, LANES:2 * LANES])
    v = proj[:, 2 * LANES:3 * LANES]
    og = proj[:, 3 * LANES:4 * LANES]
    pos = lax.broadcasted_iota(jnp.int32, f.shape, 0) % HGRN_CHUNK
    return q, 1.0 - f, v, og, _chunk_cumsum(jnp.log(f), pos)


def _hgrn_out(o, og, ng):
    return (_rms(o, ng) * (og * _sigmoid(og))).astype(_BF)


def _hgrn_wcast_kernel(*refs):
    out_ref = refs[-1]
    for s, w_ref in enumerate(refs[:-1]):
        out_ref[:, s * LANES:(s + 1) * LANES] = w_ref[...].astype(_BF)


def _hgrn_weights(w_in):
    D = w_in.shape[0]
    H = D // LANES
    hp = HGRN_HEADS_PER_STEP
    specs = [pl.BlockSpec((D, LANES), lambda p, hh=hh, g=g: (0, g * H + p * hp + hh))
             for hh in range(hp) for g in range(4)]
    return pl.pallas_call(
        _hgrn_wcast_kernel,
        grid=(H // hp,),
        in_specs=specs,
        out_specs=pl.BlockSpec((None, D, hp * 4 * LANES), lambda p: (p, 0, 0)),
        out_shape=jax.ShapeDtypeStruct((H // hp, D, hp * 4 * LANES), _BF),
        compiler_params=_params(1),
        name="hgrn_weights",
    )(*([w_in] * (4 * hp)))


def _hgrn_kernel(r_ref, gpre_ref, win_ref, lbl_ref, ng_ref, *rest, layer, tm, rb, n_cast):
    cast_in, rest = rest[:n_cast], rest[n_cast:]
    y_ref, cast_out = rest[0], rest[1:1 + n_cast]
    hn_ref, st_ref, proj_ref, o_ref, fq_ref, fb_ref = rest[1 + n_cast:]
    n = pl.program_id(1)
    p = pl.program_id(2)
    heads = range(HGRN_HEADS_PER_STEP)
    head_cols = lambda hh, w: slice(hh * w, (hh + 1) * w)

    for src_ref, dst_ref in zip(cast_in, cast_out):
        dst_ref[...] = src_ref[...].astype(_BF)

    @pl.when(p == 0)
    def _():
        def norm_rows(j, carry):
            rows = pl.ds(pl.multiple_of(j * NORM_ROWS, NORM_ROWS), NORM_ROWS)
            hn_ref[rows, :] = _rms(r_ref[rows, :], gpre_ref[...]).astype(_BF)
            return carry

        lax.fori_loop(0, tm // NORM_ROWS, norm_rows, 0)

    @pl.when(n == 0)
    def _():
        for hh in heads:
            st_ref[p * HGRN_HEADS_PER_STEP + hh] = jnp.zeros((LANES, LANES), _F32)

    lbs = []
    for hh in heads:
        lg = lbl_ref[hh]
        ex = jnp.exp(lg - jnp.max(lg, axis=0, keepdims=True))
        lbs.append(jnp.sum(ex[0:layer + 1], axis=0, keepdims=True) / jnp.sum(ex, axis=0, keepdims=True))

    n_blocks = tm // rb
    for qb in range(n_blocks):
        rows = pl.ds(qb * rb, rb)
        proj_ref[rows, :] = _dot(hn_ref[rows, :], win_ref[...])

    row = lax.broadcasted_iota(jnp.int32, (HGRN_CHUNK, HGRN_CHUNK), 0)
    col = lax.broadcasted_iota(jnp.int32, (HGRN_CHUNK, HGRN_CHUNK), 1)
    causal = row >= col

    sts = [st_ref[p * HGRN_HEADS_PER_STEP + hh] for hh in heads]
    any_unsafe = jnp.zeros((1, LANES), _F32)
    for qb in range(n_blocks):
        rows = pl.ds(qb * rb, rb)
        gates = [_hgrn_gates(proj_ref[rows, head_cols(hh, 4 * LANES)], lbs[hh]) for hh in heads]
        pairs = [(c, hh) for c in range(rb // HGRN_CHUNK) for hh in heads]
        q_mid, k_mid, q_dec, k_dec, vb, d_last = {}, {}, {}, {}, {}, {}
        for c, hh in pairs:
            sl = slice(c * HGRN_CHUNK, (c + 1) * HGRN_CHUNK)
            q, k, v, _, bcum = gates[hh]
            b = bcum[sl]
            qc, kc = q[sl], k[sl]
            b_mid = b[HGRN_CHUNK // 2 - 1:HGRN_CHUNK // 2]
            b_last = b[HGRN_CHUNK - 1:HGRN_CHUNK]
            unsafe = b_last < -HGRN_SAFE_DECAY
            any_unsafe = jnp.maximum(any_unsafe, unsafe.astype(_F32))
            q_mid[c, hh] = jnp.where(unsafe, 0.0, qc * jnp.exp(b - b_mid)).astype(_BF)
            k_mid[c, hh] = jnp.where(unsafe, 0.0, kc * jnp.exp(b_mid - b)).astype(_BF)
            q_dec[c, hh] = (qc * jnp.exp(b)).astype(_BF)
            k_dec[c, hh] = (kc * jnp.exp(b_last - b)).astype(_BF)
            vb[c, hh] = v[sl].astype(_BF)
            d_last[c, hh] = jnp.exp(b_last)
        scores = {ph: _dot_nt(q_mid[ph], k_mid[ph]) for ph in pairs}
        st_add = {ph: _dot_tn(vb[ph], k_dec[ph]) for ph in pairs}
        o_intra = {ph: _dot(jnp.where(causal, scores[ph], 0.0).astype(_BF), vb[ph]) for ph in pairs}
        st_in = {}
        for c, hh in pairs:
            st_in[c, hh] = sts[hh]
            sts[hh] = sts[hh] * d_last[c, hh] + st_add[c, hh]
        o_inter = {ph: _dot_nt(q_dec[ph], st_in[ph].astype(_BF)) for ph in pairs}
        for hh in heads:
            o = jnp.concatenate([o_intra[c, hh] + o_inter[c, hh] for c in range(rb // HGRN_CHUNK)], axis=0)
            o_ref[hh, rows, :] = o
            y_ref[rows, head_cols(hh, LANES)] = _hgrn_out(o, gates[hh][3], ng_ref[...])
    for hh in heads:
        st_ref[p * HGRN_HEADS_PER_STEP + hh] = sts[hh]

    @pl.when(jnp.max(any_unsafe) > 0.0)
    def _():
        s_idx = lax.broadcasted_iota(jnp.int32, (HGRN_CHUNK, 1), 0)

        for hh in heads:
            def chunk_body(c, carry, hh=hh):
                base = pl.multiple_of(c * HGRN_CHUNK, HGRN_CHUNK)
                chunk = pl.ds(base, HGRN_CHUNK)
                q, k, v, og, b = _hgrn_gates(proj_ref[chunk, head_cols(hh, 4 * LANES)], lbs[hh])
                unsafe = b[HGRN_CHUNK - 1:HGRN_CHUNK] < -HGRN_SAFE_DECAY
                fq_ref[...] = jnp.where(unsafe, q, 0.0)
                fb_ref[...] = b

                def row_body(t, carry2):
                    qt = fq_ref[pl.ds(t, 1), :]
                    bt = fb_ref[pl.ds(t, 1), :]
                    w = qt * k * jnp.exp(jnp.minimum(bt - b, 0.0))
                    w = jnp.where(s_idx <= t, w, 0.0)
                    sc = jnp.sum(w, axis=1, keepdims=True)
                    o_ref[hh, pl.ds(base + t, 1), :] += jnp.sum(sc * v, axis=0, keepdims=True)
                    return carry2

                lax.fori_loop(0, HGRN_CHUNK, row_body, 0)
                y_ref[chunk, head_cols(hh, LANES)] = _hgrn_out(o_ref[hh, chunk, :], og, ng_ref[...])
                return carry

            lax.fori_loop(0, tm // HGRN_CHUNK, chunk_body, 0)


def _hgrn_mixer(r, g_pre, w_in_groups, lb_logits_heads, norm_g, to_cast, *, layer, batch, seq):
    T, D = r.shape
    H = D // LANES
    hp = HGRN_HEADS_PER_STEP
    tm = min(1024, seq)
    rb = min(512, tm)
    n_s = seq // tm
    n_steps = batch * n_s * (H // hp)
    step = lambda b, n, p: (b * n_s + n) * (H // hp) + p
    cast_specs = []
    for a in to_cast:
        rows = a.shape[0] // n_steps
        assert rows * n_steps == a.shape[0] and rows % 16 == 0, a.shape
        cast_specs.append(pl.BlockSpec((rows, a.shape[1]), lambda b, n, p: (step(b, n, p), 0)))
    kern = functools.partial(_hgrn_kernel, layer=layer, tm=tm, rb=rb, n_cast=len(to_cast))
    out = pl.pallas_call(
        kern,
        grid=(batch, n_s, H // hp),
        in_specs=[
            pl.BlockSpec((tm, D), lambda b, n, p: (b * n_s + n, 0)),
            _resident((1, D), lambda b, n, p: (0, 0)),
            pl.BlockSpec((None, D, hp * 4 * LANES), lambda b, n, p: (p, 0, 0)),
            pl.BlockSpec((hp,) + lb_logits_heads.shape[1:], lambda b, n, p: (p, 0, 0)),
            _resident((1, LANES), lambda b, n, p: (0, 0)),
        ] + cast_specs,
        out_specs=[pl.BlockSpec((tm, hp * LANES), lambda b, n, p: (b * n_s + n, p))] + cast_specs,
        out_shape=[jax.ShapeDtypeStruct((T, D), _BF)]
        + [jax.ShapeDtypeStruct(a.shape, _BF) for a in to_cast],
        scratch_shapes=[
            pltpu.VMEM((tm, D), _BF),
            pltpu.VMEM((H, LANES, LANES), _F32),
            pltpu.VMEM((tm, hp * 4 * LANES), _F32),
            pltpu.VMEM((hp, tm, LANES), _F32),
            pltpu.VMEM((HGRN_CHUNK, LANES), _F32),
            pltpu.VMEM((HGRN_CHUNK, LANES), _F32),
        ],
        compiler_params=_params(3),
        name="hgrn_mixer",
    )(r, g_pre, w_in_groups, lb_logits_heads, norm_g, *to_cast)
    return out[0], out[1:]


def _gelu_erf(x):
    return 0.5 * x * (1.0 + lax.erf(x * (2.0 ** -0.5)))


def _gmlp_kernel(r_ref, gpre_ref, win_ref, lng_ref, lnb_ref, ws_ref, bst_ref, y_ref, *, rb):
    tm, D = r_ref.shape
    row = lax.broadcasted_iota(jnp.int32, (GMLP_CHUNK, GMLP_CHUNK), 0)
    col = lax.broadcasted_iota(jnp.int32, (GMLP_CHUNK, GMLP_CHUNK), 1)
    causal = row >= col
    groups = range(D // LANES)
    w_s = [jnp.where(causal, ws_ref[g], 0.0).astype(_BF) for g in groups]

    for q in range(tm // rb):
        hn = _rms(r_ref[pl.ds(q * rb, rb), :], gpre_ref[...]).astype(_BF)
        u = _gelu_erf(_dot(hn, win_ref[:, 0:D]))
        v = _gelu_erf(_dot(hn, win_ref[:, D:2 * D]))
        mu = jnp.mean(v, axis=-1, keepdims=True)
        vc = v - mu
        var = jnp.mean(vc * vc, axis=-1, keepdims=True)
        vb = (vc * lax.rsqrt(var + EPS) * lng_ref[...] + lnb_ref[...]).astype(_BF)
        for g in groups:
            cs = slice(g * LANES, (g + 1) * LANES)
            bias = bst_ref[:, g:g + 1]
            for c in range(rb // GMLP_CHUNK):
                rs = slice(c * GMLP_CHUNK, (c + 1) * GMLP_CHUNK)
                mixed = _dot(w_s[g], vb[rs, cs]) + bias
                y_ref[pl.ds(q * rb + c * GMLP_CHUNK, GMLP_CHUNK), cs] = (u[rs, cs] * mixed).astype(_BF)


def _gmlp_mixer(r, g_pre, w_in, ln_g, ln_b, w_s, b_s_t, *, index):
    T, D = r.shape
    G = D // LANES
    tm = min(512, T)
    return pl.pallas_call(
        functools.partial(_gmlp_kernel, rb=min(512, tm)),
        grid=(T // tm,),
        in_specs=[
            pl.BlockSpec((tm, D), lambda i: (i, 0)),
            _resident((1, D), lambda i: (0, 0)),
            _resident((D, 2 * D), lambda i: (index, 0)),
            _resident((1, D), lambda i: (0, 0)),
            _resident((1, D), lambda i: (0, 0)),
            _resident((G, GMLP_CHUNK, GMLP_CHUNK), lambda i: (0, 0, 0)),
            _resident((GMLP_CHUNK, G), lambda i: (0, 0)),
        ],
        out_specs=pl.BlockSpec((tm, D), lambda i: (i, 0)),
        out_shape=jax.ShapeDtypeStruct((T, D), _BF),
        compiler_params=_params(1),
        name="gmlp_mixer",
    )(r, g_pre, w_in, ln_g, ln_b, w_s, b_s_t)


def _proj_kernel(y_ref, w_ref, r_ref, g_ref, out_ref):
    out_ref[...] = r_ref[...] + _rms(_dot(y_ref[...], w_ref[...]), g_ref[...])


def _proj_residual(y, w, r, g_post, *, index):
    T, D = r.shape
    tm = min(1024, T)
    return pl.pallas_call(
        _proj_kernel,
        grid=(T // tm,),
        in_specs=[
            pl.BlockSpec((tm, D), lambda i: (i, 0)),
            _resident((D, D), lambda i: (index, 0)),
            pl.BlockSpec((tm, D), lambda i: (i, 0)),
            _resident((1, D), lambda i: (0, 0)),
        ],
        out_specs=pl.BlockSpec((tm, D), lambda i: (i, 0)),
        out_shape=jax.ShapeDtypeStruct((T, D), _F32),
        compiler_params=_params(1),
        name="proj_residual",
    )(y, w, r, g_post)


def _gelu_tanh(x):
    return 0.5 * x * (1.0 + jnp.tanh((2.0 / jnp.pi) ** 0.5 * (x + 0.044715 * (x * x * x))))


def _ffn_kernel(r_ref, gpre_ref, gpost_ref, wg_ref, wv_ref, cwg_ref, cwv_ref, cbg_ref, cbv_ref, wd_ref,
                out_ref, hn_ref, hg_ref, hv_ref, tailg_ref, tailv_ref, *, tm, rb, n_chunks, tiles_per_seq):
    i = pl.program_id(0)
    c = pl.program_id(1)
    n_blocks = tm // rb

    def conv(h_ref, q, cw_ref, cb_ref):
        acc = cb_ref[...] + h_ref[q, pl.ds(HALO, rb), :] * cw_ref[CONV_W - 1:CONV_W, :]
        for j in range(1, CONV_W):
            acc = acc + h_ref[q, pl.ds(HALO - j, rb), :] * cw_ref[CONV_W - 1 - j:CONV_W - j, :]
        return acc

    def chunk_step(first, last):
        top = pl.ds(HALO - 8, 8)
        last_rows = pl.ds(HALO + rb - 8, 8)
        seq_start = i % tiles_per_seq == 0

        @pl.when(seq_start)
        def _():
            hg_ref[0, top, :] = jnp.zeros((8, hg_ref.shape[2]), _F32)
            hv_ref[0, top, :] = jnp.zeros((8, hv_ref.shape[2]), _F32)

        @pl.when(jnp.logical_not(seq_start))
        def _():
            hg_ref[0, top, :] = tailg_ref[c]
            hv_ref[0, top, :] = tailv_ref[c]

        for q in range(n_blocks):
            rows = pl.ds(q * rb, rb)
            if first:
                hn_ref[rows, :] = _rms(r_ref[rows, :], gpre_ref[...]).astype(_BF)
            for h_ref, w_ref in ((hg_ref, wg_ref), (hv_ref, wv_ref)):
                h_ref[q, pl.ds(HALO, rb), :] = _dot(hn_ref[rows, :], w_ref[...])
                if q > 0:
                    h_ref[q, top, :] = h_ref[q - 1, last_rows, :]
        tailg_ref[c] = hg_ref[n_blocks - 1, last_rows, :]
        tailv_ref[c] = hv_ref[n_blocks - 1, last_rows, :]

        for q in range(n_blocks):
            rows = pl.ds(q * rb, rb)
            gate = conv(hg_ref, q, cwg_ref, cbg_ref)
            val = conv(hv_ref, q, cwv_ref, cbv_ref)
            act = (_gelu_tanh(gate) * val).astype(_BF)
            down = _dot(act, wd_ref[...])
            if not first:
                down = out_ref[rows, :] + down
            if last:
                down = r_ref[rows, :] + _rms(down, gpost_ref[...])
            out_ref[rows, :] = down

    n_f = pl.num_programs(1)
    pl.when(c == 0)(lambda: chunk_step(True, n_chunks == 1))
    if n_chunks > 1:
        pl.when(c == n_f - 1)(lambda: chunk_step(False, True))
    if n_chunks > 2:
        pl.when(jnp.logical_and(c > 0, c < n_f - 1))(lambda: chunk_step(False, False))


def _conv_ffn(r, g_pre, g_post, w_up, conv_w, conv_b, w_down, *, layer, seq):
    T, D = r.shape
    F = w_up.shape[1] // 2
    tm = min(1024, seq)
    rb = min(512, tm)
    fc = min(512, F)
    n_f = F // fc
    kern = functools.partial(_ffn_kernel, tm=tm, rb=rb, n_chunks=n_f, tiles_per_seq=seq // tm)
    return pl.pallas_call(
        kern,
        grid=(T // tm, n_f),
        in_specs=[
            pl.BlockSpec((tm, D), lambda i, c: (i, 0)),
            _resident((1, D), lambda i, c: (0, 0)),
            _resident((1, D), lambda i, c: (0, 0)),
            pl.BlockSpec((D, fc), lambda i, c: (layer, c)),
            pl.BlockSpec((D, fc), lambda i, c: (layer, c + n_f)),
            pl.BlockSpec((CONV_W, fc), lambda i, c: (0, c)),
            pl.BlockSpec((CONV_W, fc), lambda i, c: (0, c + n_f)),
            pl.BlockSpec((1, fc), lambda i, c: (0, c)),
            pl.BlockSpec((1, fc), lambda i, c: (0, c + n_f)),
            pl.BlockSpec((fc, D), lambda i, c: (layer * n_f + c, 0)),
        ],
        out_specs=pl.BlockSpec((tm, D), lambda i, c: (i, 0)),
        out_shape=jax.ShapeDtypeStruct((T, D), _F32),
        scratch_shapes=[
            pltpu.VMEM((tm, D), _BF),
            pltpu.VMEM((tm // rb, rb + HALO, fc), _F32),
            pltpu.VMEM((tm // rb, rb + HALO, fc), _F32),
            pltpu.VMEM((n_f, 8, fc), _F32),
            pltpu.VMEM((n_f, 8, fc), _F32),
        ],
        compiler_params=_params(2),
        name="conv_ffn",
    )(r, g_pre, g_post, w_up, w_up, conv_w, conv_w, conv_b, conv_b, w_down)


def _ple_kernel(r_ref, p_ref, win_ref, wgate_ref, ge_ref, gr_ref, out_ref):
    r = r_ref[...]
    e = _rms(_dot(p_ref[...].astype(_BF), win_ref[...]), ge_ref[...])
    gate = _sigmoid(_dot(_rms(r, gr_ref[...]).astype(_BF), wgate_ref[...]))
    out_ref[...] = r + gate * e


def _ple_gate(r, p, w_in, w_gate, g_e, g_r, *, layer):
    T, D = r.shape
    P = p.shape[1]
    tm = min(1024, T)
    return pl.pallas_call(
        _ple_kernel,
        grid=(T // tm,),
        in_specs=[
            pl.BlockSpec((tm, D), lambda i: (i, 0)),
            pl.BlockSpec((tm, P), lambda i: (layer * (T // tm) + i, 0)),
            _resident((P, D), lambda i: (0, 0)),
            _resident((D, D), lambda i: (layer, 0)),
            _resident((1, D), lambda i: (0, 0)),
            _resident((1, D), lambda i: (0, 0)),
        ],
        out_specs=pl.BlockSpec((tm, D), lambda i: (i, 0)),
        out_shape=jax.ShapeDtypeStruct((T, D), _F32),
        compiler_params=_params(1),
        name="ple_gate",
    )(r, p, w_in, w_gate, g_e, g_r)


def kernel(x, p, norm_g, hgrn_w_in, hgrn_lb_logits, hgrn_norm_g, hgrn_w_out, gmlp_w_in, gmlp_ln_g,
           gmlp_ln_b, gmlp_w_s, gmlp_b_s, gmlp_w_out, ffn_w_up, ffn_conv_w, ffn_conv_b, ffn_w_down,
           ple_w_in, ple_w_gate, ple_norm_g):
    B, S, D = x.shape
    depth = p.shape[0]
    H = D // LANES
    T = B * S
    row = lambda a: a.reshape(1, -1)

    lbl = hgrn_lb_logits.astype(_F32).reshape(-1, H, LANES).transpose(1, 0, 2)

    F = ffn_w_down.shape[1]
    stacks = (ffn_w_up.reshape(depth * D, 2 * F), ffn_w_down.reshape(depth * F, D),
              hgrn_w_out.reshape(-1, D), gmlp_w_in.reshape(-1, 2 * D), gmlp_w_out.reshape(-1, D),
              ple_w_gate.reshape(depth * D, D))
    bf = None

    r = x.reshape(T, D)
    for i in range(depth):
        j = i // 2
        if i % 2 == 0:
            y, cast = _hgrn_mixer(r, row(norm_g[i, 0]), _hgrn_weights(hgrn_w_in[j]), lbl, row(hgrn_norm_g[j]),
                                  stacks if bf is None else (), layer=i, batch=B, seq=S)
            if bf is None:
                bf = dict(zip(("ffn_up", "ffn_down", "hgrn_out", "gmlp_in", "gmlp_out", "ple_gate"), cast))
            w_out = bf["hgrn_out"]
        else:
            y = _gmlp_mixer(r, row(norm_g[i, 0]), bf["gmlp_in"], row(gmlp_ln_g[j]), row(gmlp_ln_b[j]),
                            gmlp_w_s[j], gmlp_b_s[j].T, index=j)
            w_out = bf["gmlp_out"]
        r = _proj_residual(y, w_out, r, row(norm_g[i, 1]), index=j)
        r = _conv_ffn(r, row(norm_g[i, 2]), row(norm_g[i, 3]), bf["ffn_up"], ffn_conv_w[i],
                      row(ffn_conv_b[i]), bf["ffn_down"], layer=i, seq=S)
        r = _ple_gate(r, p.reshape(depth * T, -1), ple_w_in[i].astype(_BF), bf["ple_gate"],
                      row(ple_norm_g[i, 0]), row(ple_norm_g[i, 1]), layer=i)
    return r.reshape(B, S, D)
```

```python
import functools

import jax
import jax.numpy as jnp
from jax import lax
from jax.experimental import pallas as pl
from jax.experimental.pallas import tpu as pltpu

EPS = 1e-6
LANES = 128
HGRN_CHUNK = 64
HGRN_SAFE_DECAY = 60.0
HGRN_HEADS_PER_STEP = 2
GMLP_CHUNK = 128
CONV_W = 3
SUBLANES = 8
HALO = 16
NORM_ROWS = 128
VMEM_LIMIT = 60 * 1024 * 1024

_BF = jnp.bfloat16
_F32 = jnp.float32


def _params(n_axes):
    return pltpu.CompilerParams(
        dimension_semantics=("arbitrary",) * n_axes,
        vmem_limit_bytes=VMEM_LIMIT,
    )


def _resident(shape, index_map):
    return pl.BlockSpec(shape, index_map, pipeline_mode=pl.Buffered(1))


def _rms(x, g):
    return x * lax.rsqrt(jnp.mean(x * x, axis=-1, keepdims=True) + EPS) * g


def _sigmoid(x):
    return 1.0 / (1.0 + jnp.exp(-x))


def _dot(a, b):
    return jnp.dot(a, b, preferred_element_type=_F32)


def _dot_nt(a, b):
    return lax.dot_general(a, b, (((1,), (1,)), ((), ())), preferred_element_type=_F32)


def _dot_tn(a, b):
    return lax.dot_general(a, b, (((0,), (0,)), ((), ())), preferred_element_type=_F32)


def _chunk_cumsum(x, pos):
    shift = 1
    while shift < HGRN_CHUNK:
        x = x + jnp.where(pos >= shift, pltpu.roll(x, shift, axis=0), 0.0)
        shift *= 2
    return x


def _hgrn_gates(proj, lb):
    q = proj[:, 0:LANES]
    f = lb + (1.0 - lb) * _sigmoid(proj[:, LANES:2 * LANES])
    v = proj[:, 2 * LANES:3 * LANES]
    og = proj[:, 3 * LANES:4 * LANES]
    pos = lax.broadcasted_iota(jnp.int32, f.shape, 0) % HGRN_CHUNK
    return q, 1.0 - f, v, og, _chunk_cumsum(jnp.log(f), pos)


def _hgrn_out(o, og, ng):
    return (_rms(o, ng) * (og * _sigmoid(og))).astype(_BF)


def _hgrn_wcast_kernel(*refs):
    out_ref = refs[-1]
    for s, w_ref in enumerate(refs[:-1]):
        out_ref[:, s * LANES:(s + 1) * LANES] = w_ref[...].astype(_BF)


def _hgrn_weights(w_in):
    D = w_in.shape[0]
    H = D // LANES
    hp = HGRN_HEADS_PER_STEP
    specs = [pl.BlockSpec((D, LANES), lambda p, hh=hh, g=g: (0, g * H + p * hp + hh))
             for hh in range(hp) for g in range(4)]
    return pl.pallas_call(
        _hgrn_wcast_kernel,
        grid=(H // hp,),
        in_specs=specs,
        out_specs=pl.BlockSpec((None, D, hp * 4 * LANES), lambda p: (p, 0, 0)),
        out_shape=jax.ShapeDtypeStruct((H // hp, D, hp * 4 * LANES), _BF),
        compiler_params=_params(1),
        name="hgrn_weights",
    )(*([w_in] * (4 * hp)))


def _hgrn_kernel(r_ref, gpre_ref, win_ref, lbl_ref, ng_ref, *rest, layer, tm, rb, n_cast):
    cast_in, rest = rest[:n_cast], rest[n_cast:]
    y_ref, cast_out = rest[0], rest[1:1 + n_cast]
    hn_ref, st_ref, proj_ref, o_ref, fq_ref, fb_ref = rest[1 + n_cast:]
    n = pl.program_id(1)
    p = pl.program_id(2)
    heads = range(HGRN_HEADS_PER_STEP)
    head_cols = lambda hh, w: slice(hh * w, (hh + 1) * w)

    for src_ref, dst_ref in zip(cast_in, cast_out):
        dst_ref[...] = src_ref[...].astype(_BF)

    @pl.when(p == 0)
    def _():
        def norm_rows(j, carry):
            rows = pl.ds(pl.multiple_of(j * NORM_ROWS, NORM_ROWS), NORM_ROWS)
            hn_ref[rows, :] = _rms(r_ref[rows, :], gpre_ref[...]).astype(_BF)
            return carry

        lax.fori_loop(0, tm // NORM_ROWS, norm_rows, 0)

    @pl.when(n == 0)
    def _():
        for hh in heads:
            st_ref[p * HGRN_HEADS_PER_STEP + hh] = jnp.zeros((LANES, LANES), _F32)

    lbs = []
    for hh in heads:
        lg = lbl_ref[hh]
        ex = jnp.exp(lg - jnp.max(lg, axis=0, keepdims=True))
        lbs.append(jnp.sum(ex[0:layer + 1], axis=0, keepdims=True) / jnp.sum(ex, axis=0, keepdims=True))

    n_blocks = tm // rb
    for qb in range(n_blocks):
        rows = pl.ds(qb * rb, rb)
        proj_ref[rows, :] = _dot(hn_ref[rows, :], win_ref[...])

    row = lax.broadcasted_iota(jnp.int32, (HGRN_CHUNK, HGRN_CHUNK), 0)
    col = lax.broadcasted_iota(jnp.int32, (HGRN_CHUNK, HGRN_CHUNK), 1)
    causal = row >= col

    sts = [st_ref[p * HGRN_HEADS_PER_STEP + hh] for hh in heads]
    any_unsafe = jnp.zeros((1, LANES), _F32)
    for qb in range(n_blocks):
        rows = pl.ds(qb * rb, rb)
        gates = [_hgrn_gates(proj_ref[rows, head_cols(hh, 4 * LANES)], lbs[hh]) for hh in heads]
        pairs = [(c, hh) for c in range(rb // HGRN_CHUNK) for hh in heads]
        q_mid, k_mid, q_dec, k_dec, vb, d_last = {}, {}, {}, {}, {}, {}
        for c, hh in pairs:
            sl = slice(c * HGRN_CHUNK, (c + 1) * HGRN_CHUNK)
            q, k, v, _, bcum = gates[hh]
            b = bcum[sl]
            qc, kc = q[sl], k[sl]
            b_mid = b[HGRN_CHUNK // 2 - 1:HGRN_CHUNK // 2]
            b_last = b[HGRN_CHUNK - 1:HGRN_CHUNK]
            unsafe = b_last < -HGRN_SAFE_DECAY
            any_unsafe = jnp.maximum(any_unsafe, unsafe.astype(_F32))
            q_mid[c, hh] = jnp.where(unsafe, 0.0, qc * jnp.exp(b - b_mid)).astype(_BF)
            k_mid[c, hh] = jnp.where(unsafe, 0.0, kc * jnp.exp(b_mid - b)).astype(_BF)
            q_dec[c, hh] = (qc * jnp.exp(b)).astype(_BF)
            k_dec[c, hh] = (kc * jnp.exp(b_last - b)).astype(_BF)
            vb[c, hh] = v[sl].astype(_BF)
            d_last[c, hh] = jnp.exp(b_last)
        scores = {ph: _dot_nt(q_mid[ph], k_mid[ph]) for ph in pairs}
        st_add = {ph: _dot_tn(vb[ph], k_dec[ph]) for ph in pairs}
        o_intra = {ph: _dot(jnp.where(causal, scores[ph], 0.0).astype(_BF), vb[ph]) for ph in pairs}
        st_in = {}
        for c, hh in pairs:
            st_in[c, hh] = sts[hh]
            sts[hh] = sts[hh] * d_last[c, hh] + st_add[c, hh]
        o_inter = {ph: _dot_nt(q_dec[ph], st_in[ph].astype(_BF)) for ph in pairs}
        for hh in heads:
            o = jnp.concatenate([o_intra[c, hh] + o_inter[c, hh] for c in range(rb // HGRN_CHUNK)], axis=0)
            o_ref[hh, rows, :] = o
            y_ref[rows, head_cols(hh, LANES)] = _hgrn_out(o, gates[hh][3], ng_ref[...])
    for hh in heads:
        st_ref[p * HGRN_HEADS_PER_STEP + hh] = sts[hh]

    @pl.when(jnp.max(any_unsafe) > 0.0)
    def _():
        s_idx = lax.broadcasted_iota(jnp.int32, (HGRN_CHUNK, 1), 0)

        for hh in heads:
            def chunk_body(c, carry, hh=hh):
                base = pl.multiple_of(c * HGRN_CHUNK, HGRN_CHUNK)
                chunk = pl.ds(base, HGRN_CHUNK)
                q, k, v, og, b = _hgrn_gates(proj_ref[chunk, head_cols(hh, 4 * LANES)], lbs[hh])
                unsafe = b[HGRN_CHUNK - 1:HGRN_CHUNK] < -HGRN_SAFE_DECAY
                fq_ref[...] = jnp.where(unsafe, q, 0.0)
                fb_ref[...] = b

                def row_body(t, carry2):
                    qt = fq_ref[pl.ds(t, 1), :]
                    bt = fb_ref[pl.ds(t, 1), :]
                    w = qt * k * jnp.exp(jnp.minimum(bt - b, 0.0))
                    w = jnp.where(s_idx <= t, w, 0.0)
                    sc = jnp.sum(w, axis=1, keepdims=True)
                    o_ref[hh, pl.ds(base + t, 1), :] += jnp.sum(sc * v, axis=0, keepdims=True)
                    return carry2

                lax.fori_loop(0, HGRN_CHUNK, row_body, 0)
                y_ref[chunk, head_cols(hh, LANES)] = _hgrn_out(o_ref[hh, chunk, :], og, ng_ref[...])
                return carry

            lax.fori_loop(0, tm // HGRN_CHUNK, chunk_body, 0)


def _hgrn_mixer(r, g_pre, w_in_groups, lb_logits_heads, norm_g, to_cast, *, layer, batch, seq):
    T, D = r.shape
    H = D // LANES
    hp = HGRN_HEADS_PER_STEP
    tm = min(1024, seq)
    rb = min(512, tm)
    n_s = seq // tm
    n_steps = batch * n_s * (H // hp)
    step = lambda b, n, p: (b * n_s + n) * (H // hp) + p
    cast_specs = []
    for a in to_cast:
        rows = a.shape[0] // n_steps
        assert rows * n_steps == a.shape[0] and rows % 16 == 0, a.shape
        cast_specs.append(pl.BlockSpec((rows, a.shape[1]), lambda b, n, p: (step(b, n, p), 0)))
    kern = functools.partial(_hgrn_kernel, layer=layer, tm=tm, rb=rb, n_cast=len(to_cast))
    out = pl.pallas_call(
        kern,
        grid=(batch, n_s, H // hp),
        in_specs=[
            pl.BlockSpec((tm, D), lambda b, n, p: (b * n_s + n, 0)),
            _resident((1, D), lambda b, n, p: (0, 0)),
            pl.BlockSpec((None, D, hp * 4 * LANES), lambda b, n, p: (p, 0, 0)),
            pl.BlockSpec((hp,) + lb_logits_heads.shape[1:], lambda b, n, p: (p, 0, 0)),
            _resident((1, LANES), lambda b, n, p: (0, 0)),
        ] + cast_specs,
        out_specs=[pl.BlockSpec((tm, hp * LANES), lambda b, n, p: (b * n_s + n, p))] + cast_specs,
        out_shape=[jax.ShapeDtypeStruct((T, D), _BF)]
        + [jax.ShapeDtypeStruct(a.shape, _BF) for a in to_cast],
        scratch_shapes=[
            pltpu.VMEM((tm, D), _BF),
            pltpu.VMEM((H, LANES, LANES), _F32),
            pltpu.VMEM((tm, hp * 4 * LANES), _F32),
            pltpu.VMEM((hp, tm, LANES), _F32),
            pltpu.VMEM((HGRN_CHUNK, LANES), _F32),
            pltpu.VMEM((HGRN_CHUNK, LANES), _F32),
        ],
        compiler_params=_params(3),
        name="hgrn_mixer",
    )(r, g_pre, w_in_groups, lb_logits_heads, norm_g, *to_cast)
    return out[0], out[1:]


def _gelu_erf(x):
    return 0.5 * x * (1.0 + lax.erf(x * (2.0 ** -0.5)))


def _gmlp_kernel(r_ref, gpre_ref, win_ref, lng_ref, lnb_ref, ws_ref, bst_ref, y_ref, *, rb):
    tm, D = r_ref.shape
    row = lax.broadcasted_iota(jnp.int32, (GMLP_CHUNK, GMLP_CHUNK), 0)
    col = lax.broadcasted_iota(jnp.int32, (GMLP_CHUNK, GMLP_CHUNK), 1)
    causal = row >= col
    groups = range(D // LANES)
    w_s = [jnp.where(causal, ws_ref[g], 0.0).astype(_BF) for g in groups]

    for q in range(tm // rb):
        hn = _rms(r_ref[pl.ds(q * rb, rb), :], gpre_ref[...]).astype(_BF)
        u = _gelu_erf(_dot(hn, win_ref[:, 0:D]))
        v = _gelu_erf(_dot(hn, win_ref[:, D:2 * D]))
        mu = jnp.mean(v, axis=-1, keepdims=True)
        vc = v - mu
        var = jnp.mean(vc * vc, axis=-1, keepdims=True)
        vb = (vc * lax.rsqrt(var + EPS) * lng_ref[...] + lnb_ref[...]).astype(_BF)
        for g in groups:
            cs = slice(g * LANES, (g + 1) * LANES)
            bias = bst_ref[:, g:g + 1]
            for c in range(rb // GMLP_CHUNK):
                rs = slice(c * GMLP_CHUNK, (c + 1) * GMLP_CHUNK)
                mixed = _dot(w_s[g], vb[rs, cs]) + bias
                y_ref[pl.ds(q * rb + c * GMLP_CHUNK, GMLP_CHUNK), cs] = (u[rs, cs] * mixed).astype(_BF)


def _gmlp_mixer(r, g_pre, w_in, ln_g, ln_b, w_s, b_s_t, *, index):
    T, D = r.shape
    G = D // LANES
    tm = min(512, T)
    return pl.pallas_call(
        functools.partial(_gmlp_kernel, rb=min(512, tm)),
        grid=(T // tm,),
        in_specs=[
            pl.BlockSpec((tm, D), lambda i: (i, 0)),
            _resident((1, D), lambda i: (0, 0)),
            _resident((D, 2 * D), lambda i: (index, 0)),
            _resident((1, D), lambda i: (0, 0)),
            _resident((1, D), lambda i: (0, 0)),
            _resident((G, GMLP_CHUNK, GMLP_CHUNK), lambda i: (0, 0, 0)),
            _resident((GMLP_CHUNK, G), lambda i: (0, 0)),
        ],
        out_specs=pl.BlockSpec((tm, D), lambda i: (i, 0)),
        out_shape=jax.ShapeDtypeStruct((T, D), _BF),
        compiler_params=_params(1),
        name="gmlp_mixer",
    )(r, g_pre, w_in, ln_g, ln_b, w_s, b_s_t)


def _proj_kernel(y_ref, w_ref, r_ref, g_ref, out_ref):
    out_ref[...] = r_ref[...] + _rms(_dot(y_ref[...], w_ref[...]), g_ref[...])


def _proj_residual(y, w, r, g_post, *, index):
    T, D = r.shape
    tm = min(1024, T)
    return pl.pallas_call(
        _proj_kernel,
        grid=(T // tm,),
        in_specs=[
            pl.BlockSpec((tm, D), lambda i: (i, 0)),
            _resident((D, D), lambda i: (index, 0)),
            pl.BlockSpec((tm, D), lambda i: (i, 0)),
            _resident((1, D), lambda i: (0, 0)),
        ],
        out_specs=pl.BlockSpec((tm, D), lambda i: (i, 0)),
        out_shape=jax.ShapeDtypeStruct((T, D), _F32),
        compiler_params=_params(1),
        name="proj_residual",
    )(y, w, r, g_post)


def _gelu_tanh(x):
    return 0.5 * x * (1.0 + jnp.tanh((2.0 / jnp.pi) ** 0.5 * (x + 0.044715 * (x * x * x))))


def _ffn_kernel(r_ref, gpre_ref, gpost_ref, wg_ref, wv_ref, cwg_ref, cwv_ref, cbg_ref, cbv_ref, wd_ref,
                out_ref, hn_ref, hg_ref, hv_ref, tailg_ref, tailv_ref, *, tm, rb, n_chunks, tiles_per_seq):
    i = pl.program_id(0)
    c = pl.program_id(1)
    n_blocks = tm // rb

    def conv(h_ref, q, cw_ref, cb_ref):
        acc = cb_ref[...] + h_ref[q, pl.ds(HALO, rb), :] * cw_ref[CONV_W - 1:CONV_W, :]
        for j in range(1, CONV_W):
            acc = acc + h_ref[q, pl.ds(HALO - j, rb), :] * cw_ref[CONV_W - 1 - j:CONV_W - j, :]
        return acc

    def chunk_step(first, last):
        top = pl.ds(HALO - SUBLANES, SUBLANES)
        last_rows = pl.ds(HALO + rb - SUBLANES, SUBLANES)
        seq_start = i % tiles_per_seq == 0

        @pl.when(seq_start)
        def _():
            hg_ref[0, top, :] = jnp.zeros((SUBLANES, hg_ref.shape[2]), _F32)
            hv_ref[0, top, :] = jnp.zeros((SUBLANES, hv_ref.shape[2]), _F32)

        @pl.when(jnp.logical_not(seq_start))
        def _():
            hg_ref[0, top, :] = tailg_ref[c]
            hv_ref[0, top, :] = tailv_ref[c]

        for q in range(n_blocks):
            rows = pl.ds(q * rb, rb)
            if first:
                hn_ref[rows, :] = _rms(r_ref[rows, :], gpre_ref[...]).astype(_BF)
            for h_ref, w_ref in ((hg_ref, wg_ref), (hv_ref, wv_ref)):
                h_ref[q, pl.ds(HALO, rb), :] = _dot(hn_ref[rows, :], w_ref[...])
                if q > 0:
                    h_ref[q, top, :] = h_ref[q - 1, last_rows, :]
        tailg_ref[c] = hg_ref[n_blocks - 1, last_rows, :]
        tailv_ref[c] = hv_ref[n_blocks - 1, last_rows, :]

        for q in range(n_blocks):
            rows = pl.ds(q * rb, rb)
            gate = conv(hg_ref, q, cwg_ref, cbg_ref)
            val = conv(hv_ref, q, cwv_ref, cbv_ref)
            act = (_gelu_tanh(gate) * val).astype(_BF)
            down = _dot(act, wd_ref[...])
            if not first:
                down = out_ref[rows, :] + down
            if last:
                down = r_ref[rows, :] + _rms(down, gpost_ref[...])
            out_ref[rows, :] = down

    n_f = pl.num_programs(1)
    pl.when(c == 0)(lambda: chunk_step(True, n_chunks == 1))
    if n_chunks > 1:
        pl.when(c == n_f - 1)(lambda: chunk_step(False, True))
    if n_chunks > 2:
        pl.when(jnp.logical_and(c > 0, c < n_f - 1))(lambda: chunk_step(False, False))


def _conv_ffn(r, g_pre, g_post, w_up, conv_w, conv_b, w_down, *, layer, seq):
    T, D = r.shape
    F = w_up.shape[1] // 2
    tm = min(1024, seq)
    rb = min(512, tm)
    fc = min(512, F)
    n_f = F // fc
    kern = functools.partial(_ffn_kernel, tm=tm, rb=rb, n_chunks=n_f, tiles_per_seq=seq // tm)
    return pl.pallas_call(
        kern,
        grid=(T // tm, n_f),
        in_specs=[
            pl.BlockSpec((tm, D), lambda i, c: (i, 0)),
            _resident((1, D), lambda i, c: (0, 0)),
            _resident((1, D), lambda i, c: (0, 0)),
            pl.BlockSpec((D, fc), lambda i, c: (layer, c)),
            pl.BlockSpec((D, fc), lambda i, c: (layer, c + n_f)),
            pl.BlockSpec((CONV_W, fc), lambda i, c: (0, c)),
            pl.BlockSpec((CONV_W, fc), lambda i, c: (0, c + n_f)),
            pl.BlockSpec((1, fc), lambda i, c: (0, c)),
            pl.BlockSpec((1, fc), lambda i, c: (0, c + n_f)),
            pl.BlockSpec((fc, D), lambda i, c: (layer * n_f + c, 0)),
        ],
        out_specs=pl.BlockSpec((tm, D), lambda i, c: (i, 0)),
        out_shape=jax.ShapeDtypeStruct((T, D), _F32),
        scratch_shapes=[
            pltpu.VMEM((tm, D), _BF),
            pltpu.VMEM((tm // rb, rb + HALO, fc), _F32),
            pltpu.VMEM((tm // rb, rb + HALO, fc), _F32),
            pltpu.VMEM((n_f, SUBLANES, fc), _F32),
            pltpu.VMEM((n_f, SUBLANES, fc), _F32),
        ],
        compiler_params=_params(2),
        name="conv_ffn",
    )(r, g_pre, g_post, w_up, w_up, conv_w, conv_w, conv_b, conv_b, w_down)


def _ple_kernel(r_ref, p_ref, win_ref, wgate_ref, ge_ref, gr_ref, out_ref):
    r = r_ref[...]
    e = _rms(_dot(p_ref[...].astype(_BF), win_ref[...]), ge_ref[...])
    gate = _sigmoid(_dot(_rms(r, gr_ref[...]).astype(_BF), wgate_ref[...]))
    out_ref[...] = r + gate * e


def _ple_gate(r, p, w_in, w_gate, g_e, g_r, *, layer):
    T, D = r.shape
    P = p.shape[1]
    tm = min(1024, T)
    return pl.pallas_call(
        _ple_kernel,
        grid=(T // tm,),
        in_specs=[
            pl.BlockSpec((tm, D), lambda i: (i, 0)),
            pl.BlockSpec((tm, P), lambda i: (layer * (T // tm) + i, 0)),
            _resident((P, D), lambda i: (0, 0)),
            _resident((D, D), lambda i: (layer, 0)),
            _resident((1, D), lambda i: (0, 0)),
            _resident((1, D), lambda i: (0, 0)),
        ],
        out_specs=pl.BlockSpec((tm, D), lambda i: (i, 0)),
        out_shape=jax.ShapeDtypeStruct((T, D), _F32),
        compiler_params=_params(1),
        name="ple_gate",
    )(r, p, w_in, w_gate, g_e, g_r)


def kernel(x, p, norm_g, hgrn_w_in, hgrn_lb_logits, hgrn_norm_g, hgrn_w_out, gmlp_w_in, gmlp_ln_g,
           gmlp_ln_b, gmlp_w_s, gmlp_b_s, gmlp_w_out, ffn_w_up, ffn_conv_w, ffn_conv_b, ffn_w_down,
           ple_w_in, ple_w_gate, ple_norm_g):
    B, S, D = x.shape
    depth = p.shape[0]
    H = D // LANES
    T = B * S
    row = lambda a: a.reshape(1, -1)

    lbl = hgrn_lb_logits.astype(_F32).reshape(-1, H, LANES).transpose(1, 0, 2)

    F = ffn_w_down.shape[1]
    stacks = (ffn_w_up.reshape(depth * D, 2 * F), ffn_w_down.reshape(depth * F, D),
              hgrn_w_out.reshape(-1, D), gmlp_w_in.reshape(-1, 2 * D), gmlp_w_out.reshape(-1, D),
              ple_w_gate.reshape(depth * D, D))
    bf = None

    r = x.reshape(T, D)
    for i in range(depth):
        j = i // 2
        if i % 2 == 0:
            y, cast = _hgrn_mixer(r, row(norm_g[i, 0]), _hgrn_weights(hgrn_w_in[j]), lbl, row(hgrn_norm_g[j]),
                                  stacks if bf is None else (), layer=i, batch=B, seq=S)
            if bf is None:
                bf = dict(zip(("ffn_up", "ffn_down", "hgrn_out", "gmlp_in", "gmlp_out", "ple_gate"), cast))
            w_out = bf["hgrn_out"]
        else:
            y = _gmlp_mixer(r, row(norm_g[i, 0]), bf["gmlp_in"], row(gmlp_ln_g[j]), row(gmlp_ln_b[j]),
                            gmlp_w_s[j], gmlp_b_s[j].T, index=j)
            w_out = bf["gmlp_out"]
        r = _proj_residual(y, w_out, r, row(norm_g[i, 1]), index=j)
        r = _conv_ffn(r, row(norm_g[i, 2]), row(norm_g[i, 3]), bf["ffn_up"], ffn_conv_w[i],
                      row(ffn_conv_b[i]), bf["ffn_down"], layer=i, seq=S)
        r = _ple_gate(r, p.reshape(depth * T, -1), ple_w_in[i].astype(_BF), bf["ple_gate"],
                      row(ple_norm_g[i, 0]), row(ple_norm_g[i, 1]), layer=i)
    return r.reshape(B, S, D)
```

```python
import functools

import jax
import jax.numpy as jnp
from jax import lax
from jax.experimental import pallas as pl
from jax.experimental.pallas import tpu as pltpu

EPS = 1e-6
LANES = 128
HGRN_CHUNK = 64
HGRN_SAFE_DECAY = 60.0
HGRN_HEADS_PER_STEP = 2
GMLP_CHUNK = 128
CONV_W = 3
SUBLANES = 8
HALO = 16
NORM_ROWS = 128
VMEM_LIMIT = 60 * 1024 * 1024

_BF = jnp.bfloat16
_F32 = jnp.float32


def _params(n_axes):
    return pltpu.CompilerParams(
        dimension_semantics=("arbitrary",) * n_axes,
        vmem_limit_bytes=VMEM_LIMIT,
    )


def _resident(shape, index_map):
    return pl.BlockSpec(shape, index_map, pipeline_mode=pl.Buffered(1))


def _rms(x, g):
    return x * lax.rsqrt(jnp.mean(x * x, axis=-1, keepdims=True) + EPS) * g


def _sigmoid(x):
    return 1.0 / (1.0 + jnp.exp(-x))


def _dot(a, b):
    return jnp.dot(a, b, preferred_element_type=_F32)


def _dot_nt(a, b):
    return lax.dot_general(a, b, (((1,), (1,)), ((), ())), preferred_element_type=_F32)


def _dot_tn(a, b):
    return lax.dot_general(a, b, (((0,), (0,)), ((), ())), preferred_element_type=_F32)


def _chunk_cumsum(x, pos):
    shift = 1
    while shift < HGRN_CHUNK:
        x = x + jnp.where(pos >= shift, pltpu.roll(x, shift, axis=0), 0.0)
        shift *= 2
    return x


def _hgrn_gates(proj, lb):
    q = proj[:, 0:LANES]
    f = lb + (1.0 - lb) * _sigmoid(proj[:, LANES:2 * LANES])
    v = proj[:, 2 * LANES:3 * LANES]
    og = proj[:, 3 * LANES:4 * LANES]
    pos = lax.broadcasted_iota(jnp.int32, f.shape, 0) % HGRN_CHUNK
    return q, 1.0 - f, v, og, _chunk_cumsum(jnp.log(f), pos)


def _hgrn_out(o, og, ng):
    return (_rms(o, ng) * (og * _sigmoid(og))).astype(_BF)


def _hgrn_wcast_kernel(*refs):
    out_ref = refs[-1]
    for s, w_ref in enumerate(refs[:-1]):
        out_ref[:, s * LANES:(s + 1) * LANES] = w_ref[...].astype(_BF)


def _hgrn_weights(w_in):
    D = w_in.shape[0]
    H = D // LANES
    hp = HGRN_HEADS_PER_STEP
    specs = [pl.BlockSpec((D, LANES), lambda p, hh=hh, g=g: (0, g * H + p * hp + hh))
             for hh in range(hp) for g in range(4)]
    return pl.pallas_call(
        _hgrn_wcast_kernel,
        grid=(H // hp,),
        in_specs=specs,
        out_specs=pl.BlockSpec((None, D, hp * 4 * LANES), lambda p: (p, 0, 0)),
        out_shape=jax.ShapeDtypeStruct((H // hp, D, hp * 4 * LANES), _BF),
        compiler_params=_params(1),
        name="hgrn_weights",
    )(*([w_in] * (4 * hp)))


def _hgrn_kernel(r_ref, gpre_ref, win_ref, lbl_ref, ng_ref, *rest, layer, tm, rb, n_cast):
    cast_in, rest = rest[:n_cast], rest[n_cast:]
    y_ref, cast_out = rest[0], rest[1:1 + n_cast]
    hn_ref, st_ref, proj_ref, o_ref, fq_ref, fb_ref = rest[1 + n_cast:]
    n = pl.program_id(1)
    p = pl.program_id(2)
    heads = range(HGRN_HEADS_PER_STEP)
    head_cols = lambda hh, w: slice(hh * w, (hh + 1) * w)

    for src_ref, dst_ref in zip(cast_in, cast_out):
        dst_ref[...] = src_ref[...].astype(_BF)

    @pl.when(p == 0)
    def _():
        def norm_rows(j, carry):
            rows = pl.ds(pl.multiple_of(j * NORM_ROWS, NORM_ROWS), NORM_ROWS)
            hn_ref[rows, :] = _rms(r_ref[rows, :], gpre_ref[...]).astype(_BF)
            return carry

        lax.fori_loop(0, tm // NORM_ROWS, norm_rows, 0)

    @pl.when(n == 0)
    def _():
        for hh in heads:
            st_ref[p * HGRN_HEADS_PER_STEP + hh] = jnp.zeros((LANES, LANES), _F32)

    lbs = []
    for hh in heads:
        lg = lbl_ref[hh]
        ex = jnp.exp(lg - jnp.max(lg, axis=0, keepdims=True))
        lbs.append(jnp.sum(ex[0:layer + 1], axis=0, keepdims=True) / jnp.sum(ex, axis=0, keepdims=True))

    n_blocks = tm // rb
    for qb in range(n_blocks):
        rows = pl.ds(qb * rb, rb)
        proj_ref[rows, :] = _dot(hn_ref[rows, :], win_ref[...])

    row = lax.broadcasted_iota(jnp.int32, (HGRN_CHUNK, HGRN_CHUNK), 0)
    col = lax.broadcasted_iota(jnp.int32, (HGRN_CHUNK, HGRN_CHUNK), 1)
    causal = row >= col

    sts = [st_ref[p * HGRN_HEADS_PER_STEP + hh] for hh in heads]
    any_unsafe = jnp.zeros((1, LANES), _F32)
    for qb in range(n_blocks):
        rows = pl.ds(qb * rb, rb)
        gates = [_hgrn_gates(proj_ref[rows, head_cols(hh, 4 * LANES)], lbs[hh]) for hh in heads]
        pairs = [(c, hh) for c in range(rb // HGRN_CHUNK) for hh in heads]
        q_mid, k_mid, q_dec, k_dec, vb, d_last = {}, {}, {}, {}, {}, {}
        for c, hh in pairs:
            sl = slice(c * HGRN_CHUNK, (c + 1) * HGRN_CHUNK)
            q, k, v, _, bcum = gates[hh]
            b = bcum[sl]
            qc, kc = q[sl], k[sl]
            b_mid = b[HGRN_CHUNK // 2 - 1:HGRN_CHUNK // 2]
            b_last = b[HGRN_CHUNK - 1:HGRN_CHUNK]
            unsafe = b_last < -HGRN_SAFE_DECAY
            any_unsafe = jnp.maximum(any_unsafe, unsafe.astype(_F32))
            q_mid[c, hh] = jnp.where(unsafe, 0.0, qc * jnp.exp(b - b_mid)).astype(_BF)
            k_mid[c, hh] = jnp.where(unsafe, 0.0, kc * jnp.exp(b_mid - b)).astype(_BF)
            q_dec[c, hh] = (qc * jnp.exp(b)).astype(_BF)
            k_dec[c, hh] = (kc * jnp.exp(b_last - b)).astype(_BF)
            vb[c, hh] = v[sl].astype(_BF)
            d_last[c, hh] = jnp.exp(b_last)
        scores = {ph: _dot_nt(q_mid[ph], k_mid[ph]) for ph in pairs}
        st_add = {ph: _dot_tn(vb[ph], k_dec[ph]) for ph in pairs}
        o_intra = {ph: _dot(jnp.where(causal, scores[ph], 0.0).astype(_BF), vb[ph]) for ph in pairs}
        st_in = {}
        for c, hh in pairs:
            st_in[c, hh] = sts[hh]
            sts[hh] = sts[hh] * d_last[c, hh] + st_add[c, hh]
        o_inter = {ph: _dot_nt(q_dec[ph], st_in[ph].astype(_BF)) for ph in pairs}
        for hh in heads:
            o = jnp.concatenate([o_intra[c, hh] + o_inter[c, hh] for c in range(rb // HGRN_CHUNK)], axis=0)
            o_ref[hh, rows, :] = o
            y_ref[rows, head_cols(hh, LANES)] = _hgrn_out(o, gates[hh][3], ng_ref[...])
    for hh in heads:
        st_ref[p * HGRN_HEADS_PER_STEP + hh] = sts[hh]

    @pl.when(jnp.max(any_unsafe) > 0.0)
    def _():
        s_idx = lax.broadcasted_iota(jnp.int32, (HGRN_CHUNK, 1), 0)

        for hh in heads:
            def chunk_body(c, carry, hh=hh):
                base = pl.multiple_of(c * HGRN_CHUNK, HGRN_CHUNK)
                chunk = pl.ds(base, HGRN_CHUNK)
                q, k, v, og, b = _hgrn_gates(proj_ref[chunk, head_cols(hh, 4 * LANES)], lbs[hh])
                unsafe = b[HGRN_CHUNK - 1:HGRN_CHUNK] < -HGRN_SAFE_DECAY
                fq_ref[...] = jnp.where(unsafe, q, 0.0)
                fb_ref[...] = b

                def row_body(t, carry2):
                    qt = fq_ref[pl.ds(t, 1), :]
                    bt = fb_ref[pl.ds(t, 1), :]
                    w = qt * k * jnp.exp(jnp.minimum(bt - b, 0.0))
                    w = jnp.where(s_idx <= t, w, 0.0)
                    sc = jnp.sum(w, axis=1, keepdims=True)
                    o_ref[hh, pl.ds(base + t, 1), :] += jnp.sum(sc * v, axis=0, keepdims=True)
                    return carry2

                lax.fori_loop(0, HGRN_CHUNK, row_body, 0)
                y_ref[chunk, head_cols(hh, LANES)] = _hgrn_out(o_ref[hh, chunk, :], og, ng_ref[...])
                return carry

            lax.fori_loop(0, tm // HGRN_CHUNK, chunk_body, 0)


def _hgrn_mixer(r, g_pre, w_in_groups, lb_logits_heads, norm_g, to_cast, *, layer, batch, seq):
    T, D = r.shape
    H = D // LANES
    hp = HGRN_HEADS_PER_STEP
    tm = min(1024, seq)
    rb = min(512, tm)
    n_s = seq // tm
    n_steps = batch * n_s * (H // hp)
    step = lambda b, n, p: (b * n_s + n) * (H // hp) + p
    cast_specs = []
    for a in to_cast:
        rows = a.shape[0] // n_steps
        assert rows * n_steps == a.shape[0] and rows % 16 == 0, a.shape
        cast_specs.append(pl.BlockSpec((rows, a.shape[1]), lambda b, n, p: (step(b, n, p), 0)))
    kern = functools.partial(_hgrn_kernel, layer=layer, tm=tm, rb=rb, n_cast=len(to_cast))
    out = pl.pallas_call(
        kern,
        grid=(batch, n_s, H // hp),
        in_specs=[
            pl.BlockSpec((tm, D), lambda b, n, p: (b * n_s + n, 0)),
            _resident((1, D), lambda b, n, p: (0, 0)),
            pl.BlockSpec((None, D, hp * 4 * LANES), lambda b, n, p: (p, 0, 0)),
            pl.BlockSpec((hp,) + lb_logits_heads.shape[1:], lambda b, n, p: (p, 0, 0)),
            _resident((1, LANES), lambda b, n, p: (0, 0)),
        ] + cast_specs,
        out_specs=[pl.BlockSpec((tm, hp * LANES), lambda b, n, p: (b * n_s + n, p))] + cast_specs,
        out_shape=[jax.ShapeDtypeStruct((T, D), _BF)]
        + [jax.ShapeDtypeStruct(a.shape, _BF) for a in to_cast],
        scratch_shapes=[
            pltpu.VMEM((tm, D), _BF),
            pltpu.VMEM((H, LANES, LANES), _F32),
            pltpu.VMEM((tm, hp * 4 * LANES), _F32),
            pltpu.VMEM((hp, tm, LANES), _F32),
            pltpu.VMEM((HGRN_CHUNK, LANES), _F32),
            pltpu.VMEM((HGRN_CHUNK, LANES), _F32),
        ],
        compiler_params=_params(3),
        name="hgrn_mixer",
    )(r, g_pre, w_in_groups, lb_logits_heads, norm_g, *to_cast)
    return out[0], out[1:]


def _gelu_erf(x):
    return 0.5 * x * (1.0 + lax.erf(x * (2.0 ** -0.5)))


def _gmlp_kernel(r_ref, gpre_ref, win_ref, lng_ref, lnb_ref, ws_ref, bst_ref, y_ref, *, rb):
    tm, D = r_ref.shape
    row = lax.broadcasted_iota(jnp.int32, (GMLP_CHUNK, GMLP_CHUNK), 0)
    col = lax.broadcasted_iota(jnp.int32, (GMLP_CHUNK, GMLP_CHUNK), 1)
    causal = row >= col
    groups = range(D // LANES)
    w_s = [jnp.where(causal, ws_ref[g], 0.0).astype(_BF) for g in groups]

    for q in range(tm // rb):
        hn = _rms(r_ref[pl.ds(q * rb, rb), :], gpre_ref[...]).astype(_BF)
        u = _gelu_erf(_dot(hn, win_ref[:, 0:D]))
        v = _gelu_erf(_dot(hn, win_ref[:, D:2 * D]))
        mu = jnp.mean(v, axis=-1, keepdims=True)
        vc = v - mu
        var = jnp.mean(vc * vc, axis=-1, keepdims=True)
        vb = (vc * lax.rsqrt(var + EPS) * lng_ref[...] + lnb_ref[...]).astype(_BF)
        for g in groups:
            cs = slice(g * LANES, (g + 1) * LANES)
            bias = bst_ref[:, g:g + 1]
            chunks = [slice(c * GMLP_CHUNK, (c + 1) * GMLP_CHUNK) for c in range(rb // GMLP_CHUNK)]
            mixed = _dot(w_s[g], jnp.concatenate([vb[rs, cs] for rs in chunks], axis=1))
            for c, rs in enumerate(chunks):
                m = mixed[:, c * LANES:(c + 1) * LANES] + bias
                y_ref[pl.ds(q * rb + c * GMLP_CHUNK, GMLP_CHUNK), cs] = (u[rs, cs] * m).astype(_BF)


def _gmlp_mixer(r, g_pre, w_in, ln_g, ln_b, w_s, b_s_t, *, index):
    T, D = r.shape
    G = D // LANES
    tm = min(1024, T)
    return pl.pallas_call(
        functools.partial(_gmlp_kernel, rb=min(512, tm)),
        grid=(T // tm,),
        in_specs=[
            pl.BlockSpec((tm, D), lambda i: (i, 0)),
            _resident((1, D), lambda i: (0, 0)),
            _resident((D, 2 * D), lambda i: (index, 0)),
            _resident((1, D), lambda i: (0, 0)),
            _resident((1, D), lambda i: (0, 0)),
            _resident((G, GMLP_CHUNK, GMLP_CHUNK), lambda i: (0, 0, 0)),
            _resident((GMLP_CHUNK, G), lambda i: (0, 0)),
        ],
        out_specs=pl.BlockSpec((tm, D), lambda i: (i, 0)),
        out_shape=jax.ShapeDtypeStruct((T, D), _BF),
        compiler_params=_params(1),
        name="gmlp_mixer",
    )(r, g_pre, w_in, ln_g, ln_b, w_s, b_s_t)


def _proj_kernel(y_ref, w_ref, r_ref, g_ref, out_ref):
    out_ref[...] = r_ref[...] + _rms(_dot(y_ref[...], w_ref[...]), g_ref[...])


def _proj_residual(y, w, r, g_post, *, index):
    T, D = r.shape
    tm = min(1024, T)
    return pl.pallas_call(
        _proj_kernel,
        grid=(T // tm,),
        in_specs=[
            pl.BlockSpec((tm, D), lambda i: (i, 0)),
            _resident((D, D), lambda i: (index, 0)),
            pl.BlockSpec((tm, D), lambda i: (i, 0)),
            _resident((1, D), lambda i: (0, 0)),
        ],
        out_specs=pl.BlockSpec((tm, D), lambda i: (i, 0)),
        out_shape=jax.ShapeDtypeStruct((T, D), _F32),
        compiler_params=_params(1),
        name="proj_residual",
    )(y, w, r, g_post)


def _gelu_tanh(x):
    return 0.5 * x * (1.0 + jnp.tanh((2.0 / jnp.pi) ** 0.5 * (x + 0.044715 * (x * x * x))))


def _ffn_kernel(r_ref, gpre_ref, gpost_ref, wg_ref, wv_ref, cwg_ref, cwv_ref, cbg_ref, cbv_ref, wd_ref,
                out_ref, hn_ref, hg_ref, hv_ref, tailg_ref, tailv_ref, *, tm, rb, n_chunks, tiles_per_seq):
    i = pl.program_id(0)
    c = pl.program_id(1)
    n_blocks = tm // rb

    def conv(h_ref, q, cw_ref, cb_ref):
        acc = cb_ref[...] + h_ref[q, pl.ds(HALO, rb), :] * cw_ref[CONV_W - 1:CONV_W, :]
        for j in range(1, CONV_W):
            acc = acc + h_ref[q, pl.ds(HALO - j, rb), :] * cw_ref[CONV_W - 1 - j:CONV_W - j, :]
        return acc

    def chunk_step(first, last):
        top = pl.ds(HALO - SUBLANES, SUBLANES)
        last_rows = pl.ds(HALO + rb - SUBLANES, SUBLANES)
        seq_start = i % tiles_per_seq == 0

        @pl.when(seq_start)
        def _():
            hg_ref[0, top, :] = jnp.zeros((SUBLANES, hg_ref.shape[2]), _F32)
            hv_ref[0, top, :] = jnp.zeros((SUBLANES, hv_ref.shape[2]), _F32)

        @pl.when(jnp.logical_not(seq_start))
        def _():
            hg_ref[0, top, :] = tailg_ref[c]
            hv_ref[0, top, :] = tailv_ref[c]

        for q in range(n_blocks):
            rows = pl.ds(q * rb, rb)
            if first:
                hn_ref[rows, :] = _rms(r_ref[rows, :], gpre_ref[...]).astype(_BF)
            for h_ref, w_ref in ((hg_ref, wg_ref), (hv_ref, wv_ref)):
                h_ref[q, pl.ds(HALO, rb), :] = _dot(hn_ref[rows, :], w_ref[...])
                if q > 0:
                    h_ref[q, top, :] = h_ref[q - 1, last_rows, :]
        tailg_ref[c] = hg_ref[n_blocks - 1, last_rows, :]
        tailv_ref[c] = hv_ref[n_blocks - 1, last_rows, :]

        for q in range(n_blocks):
            rows = pl.ds(q * rb, rb)
            gate = conv(hg_ref, q, cwg_ref, cbg_ref)
            val = conv(hv_ref, q, cwv_ref, cbv_ref)
            act = (_gelu_tanh(gate) * val).astype(_BF)
            down = _dot(act, wd_ref[...])
            if not first:
                down = out_ref[rows, :] + down
            if last:
                down = r_ref[rows, :] + _rms(down, gpost_ref[...])
            out_ref[rows, :] = down

    n_f = pl.num_programs(1)
    pl.when(c == 0)(lambda: chunk_step(True, n_chunks == 1))
    if n_chunks > 1:
        pl.when(c == n_f - 1)(lambda: chunk_step(False, True))
    if n_chunks > 2:
        pl.when(jnp.logical_and(c > 0, c < n_f - 1))(lambda: chunk_step(False, False))


def _conv_ffn(r, g_pre, g_post, w_up, conv_w, conv_b, w_down, *, layer, seq):
    T, D = r.shape
    F = w_up.shape[1] // 2
    tm = min(1024, seq)
    rb = min(512, tm)
    fc = min(512, F)
    n_f = F // fc
    kern = functools.partial(_ffn_kernel, tm=tm, rb=rb, n_chunks=n_f, tiles_per_seq=seq // tm)
    return pl.pallas_call(
        kern,
        grid=(T // tm, n_f),
        in_specs=[
            pl.BlockSpec((tm, D), lambda i, c: (i, 0)),
            _resident((1, D), lambda i, c: (0, 0)),
            _resident((1, D), lambda i, c: (0, 0)),
            pl.BlockSpec((D, fc), lambda i, c: (layer, c)),
            pl.BlockSpec((D, fc), lambda i, c: (layer, c + n_f)),
            pl.BlockSpec((CONV_W, fc), lambda i, c: (0, c)),
            pl.BlockSpec((CONV_W, fc), lambda i, c: (0, c + n_f)),
            pl.BlockSpec((1, fc), lambda i, c: (0, c)),
            pl.BlockSpec((1, fc), lambda i, c: (0, c + n_f)),
            pl.BlockSpec((fc, D), lambda i, c: (layer * n_f + c, 0)),
        ],
        out_specs=pl.BlockSpec((tm, D), lambda i, c: (i, 0)),
        out_shape=jax.ShapeDtypeStruct((T, D), _F32),
        scratch_shapes=[
            pltpu.VMEM((tm, D), _BF),
            pltpu.VMEM((tm // rb, rb + HALO, fc), _F32),
            pltpu.VMEM((tm // rb, rb + HALO, fc), _F32),
            pltpu.VMEM((n_f, SUBLANES, fc), _F32),
            pltpu.VMEM((n_f, SUBLANES, fc), _F32),
        ],
        compiler_params=_params(2),
        name="conv_ffn",
    )(r, g_pre, g_post, w_up, w_up, conv_w, conv_w, conv_b, conv_b, w_down)


def _ple_kernel(r_ref, p_ref, win_ref, wgate_ref, ge_ref, gr_ref, out_ref):
    r = r_ref[...]
    e = _rms(_dot(p_ref[...].astype(_BF), win_ref[...]), ge_ref[...])
    gate = _sigmoid(_dot(_rms(r, gr_ref[...]).astype(_BF), wgate_ref[...]))
    out_ref[...] = r + gate * e


def _ple_gate(r, p, w_in, w_gate, g_e, g_r, *, layer):
    T, D = r.shape
    P = p.shape[1]
    tm = min(1024, T)
    return pl.pallas_call(
        _ple_kernel,
        grid=(T // tm,),
        in_specs=[
            pl.BlockSpec((tm, D), lambda i: (i, 0)),
            pl.BlockSpec((tm, P), lambda i: (layer * (T // tm) + i, 0)),
            _resident((P, D), lambda i: (0, 0)),
            _resident((D, D), lambda i: (layer, 0)),
            _resident((1, D), lambda i: (0, 0)),
            _resident((1, D), lambda i: (0, 0)),
        ],
        out_specs=pl.BlockSpec((tm, D), lambda i: (i, 0)),
        out_shape=jax.ShapeDtypeStruct((T, D), _F32),
        compiler_params=_params(1),
        name="ple_gate",
    )(r, p, w_in, w_gate, g_e, g_r)


def kernel(x, p, norm_g, hgrn_w_in, hgrn_lb_logits, hgrn_norm_g, hgrn_w_out, gmlp_w_in, gmlp_ln_g,
           gmlp_ln_b, gmlp_w_s, gmlp_b_s, gmlp_w_out, ffn_w_up, ffn_conv_w, ffn_conv_b, ffn_w_down,
           ple_w_in, ple_w_gate, ple_norm_g):
    B, S, D = x.shape
    depth = p.shape[0]
    H = D // LANES
    T = B * S
    row = lambda a: a.reshape(1, -1)

    lbl = hgrn_lb_logits.astype(_F32).reshape(-1, H, LANES).transpose(1, 0, 2)

    F = ffn_w_down.shape[1]
    stacks = (ffn_w_up.reshape(depth * D, 2 * F), ffn_w_down.reshape(depth * F, D),
              hgrn_w_out.reshape(-1, D), gmlp_w_in.reshape(-1, 2 * D), gmlp_w_out.reshape(-1, D),
              ple_w_gate.reshape(depth * D, D))
    bf = None

    r = x.reshape(T, D)
    for i in range(depth):
        j = i // 2
        if i % 2 == 0:
            y, cast = _hgrn_mixer(r, row(norm_g[i, 0]), _hgrn_weights(hgrn_w_in[j]), lbl, row(hgrn_norm_g[j]),
                                  stacks if bf is None else (), layer=i, batch=B, seq=S)
            if bf is None:
                bf = dict(zip(("ffn_up", "ffn_down", "hgrn_out", "gmlp_in", "gmlp_out", "ple_gate"), cast))
            w_out = bf["hgrn_out"]
        else:
            y = _gmlp_mixer(r, row(norm_g[i, 0]), bf["gmlp_in"], row(gmlp_ln_g[j]), row(gmlp_ln_b[j]),
                            gmlp_w_s[j], gmlp_b_s[j].T, index=j)
            w_out = bf["gmlp_out"]
        r = _proj_residual(y, w_out, r, row(norm_g[i, 1]), index=j)
        r = _conv_ffn(r, row(norm_g[i, 2]), row(norm_g[i, 3]), bf["ffn_up"], ffn_conv_w[i],
                      row(ffn_conv_b[i]), bf["ffn_down"], layer=i, seq=S)
        r = _ple_gate(r, p.reshape(depth * T, -1), ple_w_in[i].astype(_BF), bf["ple_gate"],
                      row(ple_norm_g[i, 0]), row(ple_norm_g[i, 1]), layer=i)
    return r.reshape(B, S, D)
```
